```python
import math
import jax, jax.numpy as jnp
from jax import lax
import numpy as np

D_MODEL = 1024
BATCH = 8
SEQ = 2048
DEPTH = 1
DEC_BATCH = 128
DEC_SEQ = 1
PAST_LEN = 8192
PAGE_SIZE = 128

H_A = 4
HD_A = 64
DV_A = 2 * HD_A
H_B = 4
HD_B = 128
H_IDX = 8
D_IDX = 64
TOPK_MAX = 256
N_BUCKETS = 32
MAX_DISTANCE = 128
Q_BLOCK = 128
PEER_HEADS = 8
N_KEYS = 128
N_EXPERTS = N_KEYS * N_KEYS
PEER_DK = 256
PEER_TOPK = 16
TOKEN_BLOCK = 128
EPS = 1e-6
NEG = -1e30

SIZES = (H_A * 2 * HD_A, H_A * 2 * HD_A, H_A * DV_A,
         H_B * HD_B, H_B * HD_B, H_B * HD_B,
         H_IDX * D_IDX, D_IDX, H_IDX,
         D_MODEL, D_MODEL)
SPLITS = tuple(int(s) for s in np.cumsum(SIZES)[:-1])
D_IN = int(sum(SIZES))

kernel_name = "hybrid_diffattn_dsa_peer_step"


def rms_norm(x, g):
    xf = x.astype(jnp.float32)
    y = xf * lax.rsqrt(jnp.mean(xf * xf, axis=-1, keepdims=True) + EPS)
    return (y * g.astype(jnp.float32)).astype(x.dtype)


def t5_bucket(dist):
    n = jnp.maximum(dist, 0)
    max_exact = N_BUCKETS // 2
    nf = jnp.maximum(n, 1).astype(jnp.float32)
    large = max_exact + (jnp.log(nf / max_exact) / math.log(MAX_DISTANCE / max_exact)
                         * (N_BUCKETS - max_exact)).astype(jnp.int32)
    large = jnp.minimum(large, N_BUCKETS - 1)
    return jnp.where(n < max_exact, n, large)


def rel_bias(table, q_pos, k_pos):
    b = table[t5_bucket(q_pos[:, None] - k_pos)].astype(jnp.float32)
    return jnp.moveaxis(b, -1, 0)


def mixer_project(h, w_in, qn_a, kn_a, qn_b, kn_b):
    z = jnp.einsum('...d,de->...e', h, w_in)
    qa, ka, va, qb, kb, vb, qi, ki, wi, ga, gb = jnp.split(z, SPLITS, axis=-1)
    lead = h.shape[:-1]
    qa = rms_norm(qa.reshape(lead + (H_A, 2, HD_A)), qn_a)
    ka = rms_norm(ka.reshape(lead + (H_A, 2, HD_A)), kn_a)
    va = va.reshape(lead + (H_A, DV_A))
    qb = rms_norm(qb.reshape(lead + (H_B, HD_B)), qn_b)
    kb = rms_norm(kb.reshape(lead + (H_B, HD_B)), kn_b)
    vb = vb.reshape(lead + (H_B, HD_B))
    qi = qi.reshape(lead + (H_IDX, D_IDX))
    wi = wi * (H_IDX ** -0.5)
    return qa, ka, va, qb, kb, vb, qi, ki, wi, ga, gb


def diff_attn(q, k, v, q_pos, k_pos, table_a, lam, lam_init, subln):
    s = jnp.einsum('qhcd,khcd->chqk', q, k, preferred_element_type=jnp.float32) * (HD_A ** -0.5)
    adm = k_pos[None, :] <= q_pos[:, None]
    s = jnp.where(adm, s + rel_bias(table_a, q_pos, k_pos)[None], NEG)
    p = jax.nn.softmax(s, axis=-1)
    a = p[0] - lam * p[1]
    o = jnp.einsum('hqk,khd->qhd', a.astype(v.dtype), v)
    o = rms_norm(o, subln) * (1.0 - lam_init)
    return o.reshape(o.shape[0], H_A * DV_A)


def indexer_topk(qi, wi, ki, q_pos, k_pos, topk):
    sc = jnp.einsum('qhd,kd->qhk', qi, ki, preferred_element_type=jnp.float32) * (D_IDX ** -0.5)
    score = jnp.einsum('qh,qhk->qk', wi.astype(jnp.float32), jax.nn.relu(sc))
    adm = k_pos[None, :] <= q_pos[:, None]
    score = jnp.where(adm, score, NEG)
    _, idx = lax.top_k(score, topk)
    valid = jnp.take_along_axis(adm, idx, axis=-1)
    return idx, valid


def sparse_attn(q, k_sel, v_sel, q_pos, idx, valid, table_b):
    s = jnp.einsum('qhd,qkhd->hqk', q, k_sel, preferred_element_type=jnp.float32) * (HD_B ** -0.5)
    s = jnp.where(valid[None], s + rel_bias(table_b, q_pos, idx), NEG)
    p = jax.nn.softmax(s, axis=-1)
    o = jnp.einsum('hqk,qkhd->qhd', p.astype(v_sel.dtype), v_sel)
    return o.reshape(o.shape[0], H_B * HD_B)


def select_rows(cache, l, pt, new_rows, idx, past_len):
    is_new = idx >= past_len
    p_idx = jnp.minimum(idx, past_len - 1)
    past = cache[l, pt[p_idx // PAGE_SIZE], p_idx % PAGE_SIZE]
    fresh = new_rows[jnp.clip(idx - past_len, 0, new_rows.shape[0] - 1)]
    return jnp.where(is_new[..., None, None], fresh, past)


def merge(oa, ob, ga, gb, w_pa, w_pb, w_out, dtype):
    m = (jax.nn.sigmoid(ga) * jnp.einsum('...e,ed->...d', oa, w_pa)
         + jax.nn.sigmoid(gb) * jnp.einsum('...e,ed->...d', ob, w_pb))
    return jnp.einsum('...d,de->...e', m, w_out).astype(dtype)


def peer_block(h, wq, sk, eu, ev):
    t = h.shape[0]
    q = jnp.einsum('td,de->te', h, wq).reshape(t, PEER_HEADS, 2, PEER_DK // 2)
    s = jnp.einsum('thcd,hcnd->thcn', q, sk, preferred_element_type=jnp.float32)
    top_s, top_i = lax.top_k(s, PEER_TOPK)
    cand_s = (top_s[:, :, 0, :, None] + top_s[:, :, 1, None, :]).reshape(t, PEER_HEADS, -1)
    cand_i = (top_i[:, :, 0, :, None] * N_KEYS + top_i[:, :, 1, None, :]).reshape(t, PEER_HEADS, -1)
    best_s, best_j = lax.top_k(cand_s, PEER_TOPK)
    experts = jnp.take_along_axis(cand_i, best_j, axis=-1)
    g = jax.nn.softmax(best_s, axis=-1)
    u = eu[experts]
    act = jax.nn.gelu(jnp.einsum('thkd,td->thk', u, h, preferred_element_type=jnp.float32))
    v = ev[experts]
    return jnp.einsum('thk,thkd->td', (g * act).astype(h.dtype), v)


def peer_ffn(h, wq, sk, eu, ev):
    lead = h.shape[:-1]
    h2 = h.reshape(-1, h.shape[-1])
    n = h2.shape[0]
    nb = -(-n // TOKEN_BLOCK)
    hp = jnp.pad(h2, ((0, nb * TOKEN_BLOCK - n), (0, 0)))
    out = lax.map(lambda blk: peer_block(blk, wq, sk, eu, ev), hp.reshape(nb, TOKEN_BLOCK, -1))
    return out.reshape(nb * TOKEN_BLOCK, -1)[:n].reshape(lead + (-1,)).astype(h.dtype)


def setup_inputs(seed: int = 0) -> dict:
    key = jax.random.key(seed)
    ks = jax.random.split(key, 32)
    f32 = jnp.float32
    n_pages = PAST_LEN // PAGE_SIZE
    used = DEC_BATCH * n_pages
    n_pool = used + max(1, used // 4)

    def nrm(k, shape, scale=1.0):
        return scale * jax.random.normal(k, shape, f32)

    def gain(k, shape):
        return 1.0 + 0.05 * jax.random.normal(k, shape, f32)

    page_table = jax.random.permutation(ks[7], n_pool)[:used].reshape(DEC_BATCH, n_pages).astype(jnp.int32)
    return {
        "x_prompt": nrm(ks[0], (BATCH, SEQ, D_MODEL)),
        "x_sample": nrm(ks[1], (DEC_BATCH, DEC_SEQ, D_MODEL)),
        "cache_diff_k": nrm(ks[2], (DEPTH, n_pool, PAGE_SIZE, H_A, 2 * HD_A)),
        "cache_diff_v": nrm(ks[3], (DEPTH, n_pool, PAGE_SIZE, H_A, DV_A)),
        "cache_dsa_k": nrm(ks[4], (DEPTH, n_pool, PAGE_SIZE, H_B, HD_B)),
        "cache_dsa_v": nrm(ks[5], (DEPTH, n_pool, PAGE_SIZE, H_B, HD_B)),
        "cache_idx_k": nrm(ks[6], (DEPTH, n_pool, PAGE_SIZE, D_IDX)),
        "page_table": page_table,
        "rel_bias_table": nrm(ks[8], (N_BUCKETS, H_A + H_B), 0.5),
        "norm_mix": gain(ks[9], (DEPTH, D_MODEL)),
        "w_in": nrm(ks[10], (DEPTH, D_MODEL, D_IN), D_MODEL ** -0.5),
        "q_norm_a": gain(ks[11], (DEPTH, HD_A)),
        "k_norm_a": gain(ks[12], (DEPTH, HD_A)),
        "q_norm_b": gain(ks[13], (DEPTH, HD_B)),
        "k_norm_b": gain(ks[14], (DEPTH, HD_B)),
        "lambda_q1": nrm(ks[15], (DEPTH, HD_A), 0.1),
        "lambda_k1": nrm(ks[16], (DEPTH, HD_A), 0.1),
        "lambda_q2": nrm(ks[17], (DEPTH, HD_A), 0.1),
        "lambda_k2": nrm(ks[18], (DEPTH, HD_A), 0.1),
        "subln_a": gain(ks[19], (DEPTH, DV_A)),
        "w_proj_a": nrm(ks[20], (DEPTH, H_A * DV_A, D_MODEL), (H_A * DV_A) ** -0.5),
        "w_proj_b": nrm(ks[21], (DEPTH, H_B * HD_B, D_MODEL), (H_B * HD_B) ** -0.5),
        "w_out": nrm(ks[22], (DEPTH, D_MODEL, D_MODEL), D_MODEL ** -0.5),
        "norm_ffn": gain(ks[23], (DEPTH, D_MODEL)),
        "w_query": nrm(ks[24], (DEPTH, D_MODEL, PEER_HEADS * PEER_DK), D_MODEL ** -0.5),
        "sub_keys": nrm(ks[25], (DEPTH, PEER_HEADS, 2, N_KEYS, PEER_DK // 2), (PEER_DK // 2) ** -0.5),
        "expert_u": nrm(ks[26], (DEPTH, N_EXPERTS, D_MODEL), D_MODEL ** -0.5),
        "expert_v": nrm(ks[27], (DEPTH, N_EXPERTS, D_MODEL), 0.5),
    }


def reference(x_prompt, x_sample, cache_diff_k, cache_diff_v, cache_dsa_k, cache_dsa_v, cache_idx_k,
              page_table, rel_bias_table, norm_mix, w_in, q_norm_a, k_norm_a, q_norm_b, k_norm_b,
              lambda_q1, lambda_k1, lambda_q2, lambda_k2, subln_a, w_proj_a, w_proj_b, w_out,
              norm_ffn, w_query, sub_keys, expert_u, expert_v):
    B, S, _ = x_prompt.shape
    DB, DS, _ = x_sample.shape
    n_pages = page_table.shape[1]
    past_len = n_pages * PAGE_SIZE
    pos_p = jnp.arange(S, dtype=jnp.int32)
    pos_sq = past_len + jnp.arange(DS, dtype=jnp.int32)
    pos_sk = jnp.arange(past_len + DS, dtype=jnp.int32)
    topk_p = min(TOPK_MAX, S // 4)
    topk_s = min(TOPK_MAX, (past_len + DS) // 4)
    nblk = S // Q_BLOCK
    tab_a = rel_bias_table[:, :H_A]
    tab_b = rel_bias_table[:, H_A:]
    xp, xs = x_prompt, x_sample
    new_p, new_s = [], []
    for l in range(DEPTH):
        lam_init = 0.8 - 0.6 * math.exp(-0.3 * l)
        lam = (jnp.exp(jnp.sum(lambda_q1[l].astype(jnp.float32) * lambda_k1[l].astype(jnp.float32)))
               - jnp.exp(jnp.sum(lambda_q2[l].astype(jnp.float32) * lambda_k2[l].astype(jnp.float32)))
               + lam_init)

        hp = rms_norm(xp, norm_mix[l])
        qa, ka, va, qb, kb, vb, qi, ki, wi, ga, gb = mixer_project(
            hp, w_in[l], q_norm_a[l], k_norm_a[l], q_norm_b[l], k_norm_b[l])

        def prompt_step(i):
            b = i // nblk
            q0 = (i % nblk) * Q_BLOCK
            qpos = q0 + jnp.arange(Q_BLOCK, dtype=jnp.int32)
            sl = lambda a: lax.dynamic_slice_in_dim(a[b], q0, Q_BLOCK, axis=0)
            oa = diff_attn(sl(qa), ka[b], va[b], qpos, pos_p, tab_a, lam, lam_init, subln_a[l])
            idx, valid = indexer_topk(sl(qi), sl(wi), ki[b], qpos, pos_p, topk_p)
            kb_b, vb_b = kb[b], vb[b]
            ob = sparse_attn(sl(qb), kb_b[idx], vb_b[idx], qpos, idx, valid, tab_b)
            return oa, ob

        oa_p, ob_p = lax.map(prompt_step, jnp.arange(B * nblk, dtype=jnp.int32))
        oa_p = oa_p.reshape(B, S, H_A * DV_A)
        ob_p = ob_p.reshape(B, S, H_B * HD_B)
        xp = xp + merge(oa_p, ob_p, ga, gb, w_proj_a[l], w_proj_b[l], w_out[l], xp.dtype)
        xp = xp + peer_ffn(rms_norm(xp, norm_ffn[l]), w_query[l], sub_keys[l], expert_u[l], expert_v[l])
        new_p.append((ka.reshape(B, S, H_A, 2 * HD_A), va, kb, vb, ki))

        hs = rms_norm(xs, norm_mix[l])
        qa_s, ka_s, va_s, qb_s, kb_s, vb_s, qi_s, ki_s, wi_s, ga_s, gb_s = mixer_project(
            hs, w_in[l], q_norm_a[l], k_norm_a[l], q_norm_b[l], k_norm_b[l])

        def sample_step(b):
            pt = page_table[b]

            def past(cache):
                g = cache[l, pt]
                return g.reshape((past_len,) + g.shape[2:])

            ka_all = jnp.concatenate([past(cache_diff_k).reshape(past_len, H_A, 2, HD_A), ka_s[b]], axis=0)
            va_all = jnp.concatenate([past(cache_diff_v), va_s[b]], axis=0)
            oa = diff_attn(qa_s[b], ka_all, va_all, pos_sq, pos_sk, tab_a, lam, lam_init, subln_a[l])
            ki_all = jnp.concatenate([past(cache_idx_k), ki_s[b]], axis=0)
            idx, valid = indexer_topk(qi_s[b], wi_s[b], ki_all, pos_sq, pos_sk, topk_s)
            kb_sel = select_rows(cache_dsa_k, l, pt, kb_s[b], idx, past_len)
            vb_sel = select_rows(cache_dsa_v, l, pt, vb_s[b], idx, past_len)
            ob = sparse_attn(qb_s[b], kb_sel, vb_sel, pos_sq, idx, valid, tab_b)
            return oa, ob

        oa_s, ob_s = lax.map(sample_step, jnp.arange(DB, dtype=jnp.int32))
        xs = xs + merge(oa_s, ob_s, ga_s, gb_s, w_proj_a[l], w_proj_b[l], w_out[l], xs.dtype)
        xs = xs + peer_ffn(rms_norm(xs, norm_ffn[l]), w_query[l], sub_keys[l], expert_u[l], expert_v[l])
        new_s.append((ka_s.reshape(DB, DS, H_A, 2 * HD_A), va_s, kb_s, vb_s, ki_s))

    stk = lambda rows, j: jnp.stack([r[j] for r in rows], axis=0)
    return (xp, xs,
            stk(new_p, 0), stk(new_p, 1), stk(new_p, 2), stk(new_p, 3), stk(new_p, 4),
            stk(new_s, 0), stk(new_s, 1), stk(new_s, 2), stk(new_s, 3), stk(new_s, 4))
```

```python
import functools
import math

import numpy as np
import jax
import jax.numpy as jnp
from jax import lax
from jax.experimental import pallas as pl
from jax.experimental.pallas import tpu as pltpu

F32 = jnp.float32
BF16 = jnp.bfloat16
I32 = jnp.int32

H_A = 4
HD_A = 64
DV_A = 2 * HD_A
H_B = 4
HD_B = 128
H_IDX = 8
D_IDX = 64
TOPK_MAX = 256
N_BUCKETS = 32
MAX_DISTANCE = 128
PAGE_SIZE = 128
PEER_HEADS = 8
N_KEYS = 128
PEER_DK = 256
PEER_TOPK = 16
EPS = 1e-6
NEG = -1e30
LAM_INIT = 0.8 - 0.6 * math.exp(-0.3 * 0)
INT_MIN = -(2 ** 31)

WA = H_A * 2 * HD_A
WB = H_B * HD_B
WI = H_IDX * D_IDX
N_MAPS_A = 2 * H_A

VMEM_LIMIT = 56 * 1024 * 1024


def _cparams(sem):
    return pltpu.CompilerParams(dimension_semantics=sem, vmem_limit_bytes=VMEM_LIMIT)


def _dot(a, b):
    return jnp.dot(a, b, preferred_element_type=F32)


def _dot_nt(a, b):
    return lax.dot_general(a, b, (((1,), (1,)), ((), ())), preferred_element_type=F32)


def _rms_rows(x, g):
    return x * lax.rsqrt(jnp.mean(x * x, axis=-1, keepdims=True) + EPS) * g


def _t5_bucket_np(dist):
    n = np.maximum(dist, 0)
    max_exact = N_BUCKETS // 2
    nf = np.maximum(n, 1).astype(np.float64)
    large = max_exact + (np.log(nf / max_exact) / math.log(MAX_DISTANCE / max_exact)
                         * (N_BUCKETS - max_exact)).astype(np.int64)
    large = np.minimum(large, N_BUCKETS - 1)
    return np.where(n < max_exact, n, large).astype(np.int32)


def _to_key(x):
    b = pltpu.bitcast(x + 0.0, I32)
    return b ^ ((b >> 31) & 0x7FFFFFFF)


def _inproj_kernel(x_ref, g_ref, w_ref, gn_ref, g64_ref,
                   qa_o, kaf_o, kab_o, vaf_o, vab_o, qb_o, kbf_o, kbb_o, vbf_o, vbb_o,
                   qi_o, kif_o, kib_o, wi_o):
    x = x_ref[...]
    h = _rms_rows(x, g_ref[...]).astype(BF16)

    def proj(c0, n):
        return _dot(h, w_ref[:, c0:c0 + n])

    def rms64(z, gain):
        sq = z * z
        hi = sq.astype(BF16)
        lo = (sq - hi.astype(F32)).astype(BF16)
        ms = _dot(hi, g64_ref[...]) + _dot(lo, g64_ref[...])
        return z * lax.rsqrt(ms + EPS) * gain

    def rms128(z, gain):
        parts = []
        for hh in range(H_B):
            zz = z[:, hh * HD_B:(hh + 1) * HD_B]
            parts.append(zz * lax.rsqrt(jnp.mean(zz * zz, axis=-1, keepdims=True) + EPS))
        return jnp.concatenate(parts, axis=-1) * gain

    qa = rms64(proj(0, WA), gn_ref[0:1, :])
    qa_o[...] = (qa * (HD_A ** -0.5)).astype(BF16)
    ka = rms64(proj(WA, WA), gn_ref[1:2, :])
    kaf_o[...] = ka
    kab_o[...] = ka.astype(BF16)
    va = proj(2 * WA, WA)
    vaf_o[...] = va
    vab_o[...] = va.astype(BF16)
    c = 3 * WA
    qb = rms128(proj(c, WB), gn_ref[2:3, :])
    qb_o[...] = qb.astype(BF16)
    kb = rms128(proj(c + WB, WB), gn_ref[3:4, :])
    kbf_o[...] = kb
    kbb_o[...] = kb.astype(BF16)
    vb = proj(c + 2 * WB, WB)
    vbf_o[...] = vb
    vbb_o[...] = vb.astype(BF16)
    c = c + 3 * WB
    qi = proj(c, WI)
    qi_o[...] = (qi * (D_IDX ** -0.5)).astype(BF16)
    misc = proj(c + WI, 128)
    ki = misc[:, 0:D_IDX]
    kif_o[...] = ki
    kib_o[...] = ki.astype(BF16)
    wi_o[...] = misc[:, D_IDX:D_IDX + H_IDX] * (H_IDX ** -0.5)


def _inproj(x2, g_mix, w1, gains, g64, tt):
    t, d = x2.shape
    assert t % tt == 0
    row = lambda n: pl.BlockSpec((tt, n), lambda i: (i, 0))
    full = lambda a: pl.BlockSpec(a.shape, lambda i: (0,) * a.ndim)
    sd = lambda n, dt: jax.ShapeDtypeStruct((t, n), dt)
    out_shape = [sd(WA, BF16), sd(WA, F32), sd(WA, BF16), sd(WA, F32), sd(WA, BF16),
                 sd(WB, BF16), sd(WB, F32), sd(WB, BF16), sd(WB, F32), sd(WB, BF16),
                 sd(WI, BF16), sd(D_IDX, F32), sd(D_IDX, BF16), sd(H_IDX, F32)]
    out_specs = [row(s.shape[1]) for s in out_shape]
    return pl.pallas_call(
        _inproj_kernel,
        out_shape=out_shape,
        grid=(t // tt,),
        in_specs=[row(d), full(g_mix), full(w1), full(gains), full(g64)],
        out_specs=out_specs,
        compiler_params=_cparams(("parallel",)),
        name="inproj",
    )(x2, g_mix, w1, gains, g64)


def _lambda_value(lam_ref):
    a = jnp.sum(lam_ref[0:1, :] * lam_ref[1:2, :], axis=-1, keepdims=True)
    b = jnp.sum(lam_ref[2:3, :] * lam_ref[3:4, :], axis=-1, keepdims=True)
    return jnp.exp(a) - jnp.exp(b) + LAM_INIT


def _count_rows(pred_f32):
    n = pred_f32.shape[1] // 128
    acc = pred_f32[:, 0:128]
    for i in range(1, n):
        acc = acc + pred_f32[:, i * 128:(i + 1) * 128]
    return acc


def _prompt_attn_kernel(qa_ref, qb_ref, qi_ref, wi_ref, ka_ref, va_ref, kb_ref, vb_ref, ki_ref,
                        bias_ref, lam_ref, subln_ref, oa_ref, ob_ref,
                        keys_scr, sel_scr, m_scr, l_scr, acc_scr, *, tq, topk, idx_bits):
    tk = tq
    qblk = pl.program_id(1)
    nj = qblk + 1
    q0 = qblk * tq
    row_pos = q0 + lax.broadcasted_iota(I32, (tq, tk), 0)
    col_iota = lax.broadcasted_iota(I32, (tq, tk), 1)

    def idx_block(j, carry):
        kij = ki_ref[pl.ds(pl.multiple_of(j * tk, tk), tk), :]
        acc = jnp.zeros((tq, tk), F32)
        for hh in range(H_IDX):
            sc = _dot_nt(qi_ref[:, hh * D_IDX:(hh + 1) * D_IDX], kij)
            acc = acc + wi_ref[:, hh:hh + 1] * jnp.maximum(sc, 0.0)
        adm = (col_iota + j * tk) <= row_pos
        keys_scr[j] = _to_key(jnp.where(adm, acc, NEG))
        return carry

    lax.fori_loop(0, nj, idx_block, 0)

    def count(pred_fn):
        def body(j, acc):
            return acc + _count_rows(pred_fn(keys_scr[j], j))
        acc = lax.fori_loop(0, nj, body, jnp.zeros((tq, 128), F32))
        return jnp.sum(acc, axis=-1, keepdims=True)

    kf = float(topk)
    c0 = count(lambda k, j: jnp.where(k >= 0, 1.0, 0.0))
    t0 = jnp.where(c0 >= kf, 0, INT_MIN).astype(I32)

    def bit_step(b, t):
        cand = t + jnp.left_shift(jnp.int32(1), 30 - b)
        c = count(lambda k, j: jnp.where(k >= cand, 1.0, 0.0))
        return jnp.where(c >= kf, cand, t)

    thr = lax.fori_loop(0, 31, bit_step, t0)

    n_gt = count(lambda k, j: jnp.where(k > thr, 1.0, 0.0))
    need = kf - n_gt

    def pos_step(b, p):
        cand = p + jnp.left_shift(jnp.int32(1), idx_bits - 1 - b)
        c = count(lambda k, j: jnp.where((k == thr) & ((col_iota + j * tk) < cand), 1.0, 0.0))
        return jnp.where(c < need, cand, p)

    pos = lax.fori_loop(0, idx_bits, pos_step, jnp.zeros((tq, 1), I32))

    n_maps = N_MAPS_A + H_B
    m_scr[...] = jnp.full(m_scr.shape, NEG, F32)
    l_scr[...] = jnp.zeros(l_scr.shape, F32)
    acc_scr[...] = jnp.zeros(acc_scr.shape, F32)

    def online_update(mi, s, v):
        m_old = m_scr[mi]
        m_new = jnp.maximum(m_old, jnp.max(s, axis=-1, keepdims=True))
        alpha = jnp.exp(m_old - m_new)
        p = jnp.exp(s - m_new)
        l_scr[mi] = alpha * l_scr[mi] + jnp.sum(p, axis=-1, keepdims=True)
        acc_scr[mi] = alpha * acc_scr[mi] + _dot(p.astype(BF16), v)
        m_scr[mi] = m_new

    def attn_block(j, carry):
        kind = jnp.minimum(qblk - j, 2)
        rows = pl.ds(pl.multiple_of(j * tk, tk), tk)
        for hh in range(H_A):
            v = va_ref[rows, hh * DV_A:(hh + 1) * DV_A]
            bias = bias_ref[hh, kind]
            for c in range(2):
                col = (hh * 2 + c) * HD_A
                s = _dot_nt(qa_ref[:, col:col + HD_A], ka_ref[rows, col:col + HD_A]) + bias
                online_update(hh * 2 + c, s, v)
        key = keys_scr[j]
        sel = (key > thr) | ((key == thr) & ((col_iota + j * tk) <= pos))
        sel_scr[...] = jnp.where(sel, 1.0, 0.0)
        for hh in range(H_B):
            col = hh * HD_B
            s = _dot_nt(qb_ref[:, col:col + HD_B], kb_ref[rows, col:col + HD_B]) * (HD_B ** -0.5)
            s = jnp.where(sel_scr[...] > 0.5, s + bias_ref[H_A + hh, kind], NEG)
            online_update(N_MAPS_A + hh, s, vb_ref[rows, col:col + HD_B])
        return carry

    lax.fori_loop(0, nj, attn_block, 0)

    lam = _lambda_value(lam_ref)
    for hh in range(H_A):
        o = acc_scr[2 * hh] / l_scr[2 * hh] - lam * (acc_scr[2 * hh + 1] / l_scr[2 * hh + 1])
        o = _rms_rows(o, subln_ref[...]) * (1.0 - LAM_INIT)
        oa_ref[:, hh * DV_A:(hh + 1) * DV_A] = o.astype(oa_ref.dtype)
    for hh in range(H_B):
        o = acc_scr[N_MAPS_A + hh] / l_scr[N_MAPS_A + hh]
        ob_ref[:, hh * HD_B:(hh + 1) * HD_B] = o.astype(ob_ref.dtype)


def _prompt_bias_tiles(table, tq):
    r = np.arange(tq)[:, None]
    c = np.arange(tq)[None, :]
    tiles = []
    for kind in range(3):
        dist = kind * tq + r - c
        t = jnp.take(table, jnp.asarray(_t5_bucket_np(dist)), axis=0)
        t = jnp.where(jnp.asarray(dist >= 0)[..., None], t, NEG)
        tiles.append(jnp.moveaxis(t, -1, 0))
    return jnp.stack(tiles, axis=1).astype(F32)


def _prompt_attention(qa, qb, qi, wi, ka, va, kb, vb, ki, bias, lam4, subln, tq):
    b, s, _ = qa.shape
    assert s % tq == 0 and tq % 128 == 0
    nq = s // tq
    topk = min(TOPK_MAX, s // 4)
    idx_bits = max(1, int(math.ceil(math.log2(s))))
    qspec = lambda n: pl.BlockSpec((None, tq, n), lambda bi, qi_: (bi, qi_, 0))
    kspec = lambda n: pl.BlockSpec((None, s, n), lambda bi, qi_: (bi, 0, 0))
    full = lambda a: pl.BlockSpec(a.shape, lambda bi, qi_: (0,) * a.ndim)
    n_maps = N_MAPS_A + H_B
    kern = functools.partial(_prompt_attn_kernel, tq=tq, topk=topk, idx_bits=idx_bits)
    return pl.pallas_call(
        kern,
        out_shape=[jax.ShapeDtypeStruct((b, s, WA), BF16), jax.ShapeDtypeStruct((b, s, WB), BF16)],
        grid=(b, nq),
        in_specs=[qspec(WA), qspec(WB), qspec(WI), qspec(H_IDX),
                  kspec(WA), kspec(WA), kspec(WB), kspec(WB), kspec(D_IDX),
                  full(bias), full(lam4), full(subln)],
        out_specs=[qspec(WA), qspec(WB)],
        scratch_shapes=[pltpu.VMEM((nq, tq, tq), I32),
                        pltpu.VMEM((tq, tq), F32),
                        pltpu.VMEM((n_maps, tq, 1), F32),
                        pltpu.VMEM((n_maps, tq, 1), F32),
                        pltpu.VMEM((n_maps, tq, 128), F32)],
        compiler_params=_cparams(("parallel", "arbitrary")),
        name="prompt_attn",
    )(qa, qb, qi, wi, ka, va, kb, vb, ki, bias, lam4, subln)


def _sigmoid(x):
    return 1.0 / (1.0 + jnp.exp(-x))


def _merge_kernel(x_ref, oa_ref, ob_ref, gmix_ref, wg_ref, wpa_ref, wpb_ref, wout_ref, gffn_ref, wq_ref,
                  x1_o, h2_o, q_o):
    x = x_ref[...]
    d = x.shape[1]
    h = _rms_rows(x, gmix_ref[...]).astype(BF16)
    ga = _dot(h, wg_ref[:, 0:d])
    gb = _dot(h, wg_ref[:, d:2 * d])
    m = _sigmoid(ga) * _dot(oa_ref[...], wpa_ref[...]) + _sigmoid(gb) * _dot(ob_ref[...], wpb_ref[...])
    x1 = x + _dot(m.astype(BF16), wout_ref[...])
    x1_o[...] = x1
    h2 = _rms_rows(x1, gffn_ref[...]).astype(BF16)
    h2_o[...] = h2
    q_o[...] = _dot(h2, wq_ref[...]).astype(BF16)


def _merge(x2, oa, ob, g_mix, w_gate, w_pa, w_pb, w_o, g_ffn, w_q, tt):
    t, d = x2.shape
    assert t % tt == 0
    row = lambda n: pl.BlockSpec((tt, n), lambda i: (i, 0))
    full = lambda a: pl.BlockSpec(a.shape, lambda i: (0,) * a.ndim)
    nq = w_q.shape[1]
    return pl.pallas_call(
        _merge_kernel,
        out_shape=[jax.ShapeDtypeStruct((t, d), F32), jax.ShapeDtypeStruct((t, d), BF16),
                   jax.ShapeDtypeStruct((t, nq), BF16)],
        grid=(t // tt,),
        in_specs=[row(d), row(oa.shape[1]), row(ob.shape[1]), full(g_mix), full(w_gate), full(w_pa), full(w_pb),
                  full(w_o), full(g_ffn), full(w_q)],
        out_specs=[row(d), row(d), row(nq)],
        compiler_params=_cparams(("parallel",)),
        name="merge",
    )(x2, oa, ob, g_mix, w_gate, w_pa, w_pb, w_o, g_ffn, w_q)


PEER_STRIP = 128
NOT_SELECTED = float(PEER_TOPK)


def _extract_top(cur, order, n_out):
    lanes = cur.shape[1]
    out_row = lax.broadcasted_iota(I32, (n_out, lanes), 0)
    vals = jnp.zeros((n_out, lanes), F32)
    rank = jnp.full(cur.shape, NOT_SELECTED, F32)
    for r in range(n_out):
        mx = jnp.max(cur, axis=0, keepdims=True)
        first = jnp.min(jnp.where(cur == mx, order, 1e9), axis=0, keepdims=True)
        hit = order == first
        rank = jnp.where(hit, float(r), rank)
        cur = jnp.where(hit, -jnp.inf, cur)
        vals = jnp.where(out_row == r, mx, vals)
    return vals, rank


def _peer_select_kernel(q_ref, sk_ref, rank1_o, b_o, n_o, a_o):
    k = PEER_TOPK
    half = PEER_DK // 2
    lanes = q_ref.shape[0]
    key_idx = lax.broadcasted_iota(I32, (N_KEYS, lanes), 0).astype(F32)
    row8 = lax.broadcasted_iota(I32, (8, lanes), 0).astype(F32)
    row16 = lax.broadcasted_iota(I32, (k, lanes), 0).astype(F32)
    for hh in range(PEER_HEADS):
        c0 = hh * PEER_DK
        s0 = _dot_nt(sk_ref[hh, 0], q_ref[:, c0:c0 + half])
        s1 = _dot_nt(sk_ref[hh, 1], q_ref[:, c0 + half:c0 + PEER_DK])
        t0, rank0 = _extract_top(s0, key_idx, k)
        t1, rank1 = _extract_top(s1, key_idx, k)
        cands = [t0[0:1] + t1]
        orders = [row16]
        for r0 in range(1, 8):
            c = t0[r0:r0 + 1] + t1[0:8]
            lim = k // (r0 + 1)
            if lim < 8:
                c = jnp.where(row8 < float(lim), c, -jnp.inf)
            cands.append(c)
            orders.append(row8 + float(k * r0))
        cands.append(t0[8:16] + t1[0:1])
        orders.append((row8 + 8.0) * float(k))
        cand = jnp.concatenate(cands, axis=0)
        order = jnp.concatenate(orders, axis=0)
        _, crank = _extract_top(cand, order, k)
        chosen = crank < NOT_SELECTED
        z = jnp.sum(jnp.where(chosen, jnp.exp(cand - (t0[0:1] + t1[0:1])), 0.0), axis=0, keepdims=True)
        chosen_f = jnp.where(chosen, 1.0, 0.0)
        n_low = jnp.zeros((8, lanes), F32)
        n_low = jnp.where(row8 == 0.0, jnp.sum(chosen_f[0:16], axis=0, keepdims=True), n_low)
        for r0 in range(1, 8):
            blk = chosen_f[16 + 8 * (r0 - 1):16 + 8 * r0]
            n_low = jnp.where(row8 == float(r0), jnp.sum(blk, axis=0, keepdims=True), n_low)
        n16 = jnp.concatenate([n_low, chosen_f[72:80]], axis=0)
        n_key = jnp.zeros((N_KEYS, lanes), F32)
        for r0 in range(k):
            n_key = jnp.where(rank0 == float(r0), n16[r0:r0 + 1], n_key)
        a = jnp.where(rank0 < NOT_SELECTED, jnp.exp(s0 - t0[0:1]), 0.0)
        b = jnp.where(rank1 < NOT_SELECTED, jnp.exp(s1 - t1[0:1]), 0.0) / z
        rank1_o[hh] = rank1.astype(BF16)
        b_o[hh] = b.astype(BF16)
        n_o[hh] = n_key
        a_o[hh] = a


def _peer_select(q, sk):
    t = q.shape[0]
    assert t % PEER_STRIP == 0
    spec = pl.BlockSpec((PEER_HEADS, N_KEYS, PEER_STRIP), lambda i: (0, 0, i))
    sd = lambda dt: jax.ShapeDtypeStruct((PEER_HEADS, N_KEYS, t), dt)
    return pl.pallas_call(
        _peer_select_kernel,
        out_shape=[sd(BF16), sd(BF16), sd(F32), sd(F32)],
        grid=(t // PEER_STRIP,),
        in_specs=[pl.BlockSpec((PEER_STRIP, q.shape[1]), lambda i: (i, 0)),
                  pl.BlockSpec(sk.shape, lambda i: (0, 0, 0, 0))],
        out_specs=[spec, spec, spec, spec],
        compiler_params=_cparams(("parallel",)),
        name="peer_select",
    )(q, sk)


def _gelu_tanh(x):
    return 0.5 * x * (1.0 + jnp.tanh(math.sqrt(2.0 / math.pi) * (x + 0.044715 * (x * x * x))))


def _peer_main_kernel(h_ref, x1_ref, eu_ref, evt_ref, rank1_ref, b_ref, n_ref, a_ref, y_ref,
                      hact_scr, g_scr, acc_scr, *, rows_per_chunk):
    e = pl.program_id(1)
    tt = h_ref.shape[0]

    @pl.when(e == 0)
    def _():
        acc_scr[...] = jnp.zeros(acc_scr.shape, F32)

    hact_scr[...] = _dot_nt(eu_ref[...], h_ref[...])

    zero = jnp.zeros((), BF16)
    for st in range(tt // 128):
        cols = slice(st * 128, (st + 1) * 128)
        for grp in range(rows_per_chunk // 8):
            base = pl.multiple_of(e * rows_per_chunk + grp * 8, 8)
            n8 = [n_ref[hh, pl.ds(base, 8), cols] for hh in range(PEER_HEADS)]
            a8 = [a_ref[hh, pl.ds(base, 8), cols] for hh in range(PEER_HEADS)]
            for r in range(8):
                w = jnp.zeros((N_KEYS, 128), BF16)
                for hh in range(PEER_HEADS):
                    n_b = jnp.broadcast_to(n8[hh][r:r + 1], (N_KEYS, 128)).astype(BF16)
                    a_b = jnp.broadcast_to(a8[hh][r:r + 1], (N_KEYS, 128)).astype(BF16)
                    w = w + jnp.where(rank1_ref[hh, :, cols] < n_b, b_ref[hh, :, cols], zero) * a_b
                rows = slice((grp * 8 + r) * N_KEYS, (grp * 8 + r + 1) * N_KEYS)
                g = _gelu_tanh(hact_scr[rows, cols]) * w.astype(F32)
                g_scr[rows, cols] = g.astype(BF16)
    acc_scr[...] += _dot(evt_ref[...], g_scr[...])

    @pl.when(e == pl.num_programs(1) - 1)
    def _():
        y_ref[...] = x1_ref[...] + acc_scr[...].T


def _peer_main(h2, x1, eu, evt, rank1, b, n, a, tt, ec):
    t, d = h2.shape
    n_exp = eu.shape[0]
    assert t % tt == 0 and n_exp % ec == 0 and ec % (8 * N_KEYS) == 0
    tok = lambda: pl.BlockSpec((PEER_HEADS, N_KEYS, tt), lambda i, e: (0, 0, i))
    kern = functools.partial(_peer_main_kernel, rows_per_chunk=ec // N_KEYS)
    return pl.pallas_call(
        kern,
        out_shape=jax.ShapeDtypeStruct((t, d), F32),
        grid=(t // tt, n_exp // ec),
        in_specs=[pl.BlockSpec((tt, d), lambda i, e: (i, 0)),
                  pl.BlockSpec((tt, d), lambda i, e: (i, 0)),
                  pl.BlockSpec((ec, d), lambda i, e: (e, 0)),
                  pl.BlockSpec((d, ec), lambda i, e: (0, e)),
                  tok(), tok(), tok(), tok()],
        out_specs=pl.BlockSpec((tt, d), lambda i, e: (i, 0)),
        scratch_shapes=[pltpu.VMEM((ec, tt), F32), pltpu.VMEM((ec, tt), BF16), pltpu.VMEM((d, tt), F32)],
        compiler_params=_cparams(("parallel", "arbitrary")),
        name="peer_main",
    )(h2, x1, eu, evt, rank1, b, n, a)


PAGES_PER_STEP = 8
MAP_ROWS = 8


def _block_diag_q(q_row, group):
    w = q_row.shape[1]
    r_i = lax.broadcasted_iota(I32, (MAP_ROWS, w), 0)
    c_i = lax.broadcasted_iota(I32, (MAP_ROWS, w), 1)
    lo = r_i * group
    keep = (c_i >= lo) & (c_i < lo + group)
    return jnp.where(keep, jnp.broadcast_to(q_row.astype(F32), (MAP_ROWS, w)), 0.0).astype(q_row.dtype)


def _sum_all(x):
    return jnp.sum(jnp.sum(x, axis=0, keepdims=True), axis=1, keepdims=True)


def _sample_attn_kernel(pt_ref, *refs, group, scale, with_index, n_pages, topk, idx_bits):
    pps = PAGES_PER_STEP
    it = iter(refs)
    q_ref = next(it)
    knew_ref = next(it)
    vnew_ref = next(it)
    bias_ref = next(it)
    bias_new_ref = next(it)
    k_refs = [next(it) for _ in range(pps)]
    v_refs = [next(it) for _ in range(pps)]
    if with_index:
        qi_ref = next(it)
        wi_ref = next(it)
        kinew_ref = next(it)
        lam_ref = next(it)
        subln_ref = next(it)
        ki_refs = [next(it) for _ in range(pps)]
        o_ref = next(it)
        sel_ref = next(it)
        selnew_ref = next(it)
        m_scr, l_scr, acc_scr, score_scr = next(it), next(it), next(it), next(it)
    else:
        selin_ref = next(it)
        selnewin_ref = next(it)
        o_ref = next(it)
        m_scr, l_scr, acc_scr = next(it), next(it), next(it)

    s = pl.program_id(1)

    @pl.when(s == 0)
    def _():
        m_scr[...] = jnp.full(m_scr.shape, NEG, F32)
        l_scr[...] = jnp.zeros(l_scr.shape, F32)
        acc_scr[...] = jnp.zeros(acc_scr.shape, F32)

    qbd = _block_diag_q(q_ref[...], group)

    tiles = []
    for j in range(pps):
        st = _dot_nt(qbd, k_refs[j][...].astype(BF16))
        if scale != 1.0:
            st = st * scale
        st = st + bias_ref[j]
        if not with_index:
            st = jnp.where(selin_ref[j:j + 1, :] > 0.5, st, NEG)
        tiles.append(st)
    mx = tiles[0]
    for j in range(1, pps):
        mx = jnp.maximum(mx, tiles[j])
    m_old = m_scr[...]
    m_new = jnp.maximum(m_old, jnp.max(mx, axis=-1, keepdims=True))
    alpha = jnp.exp(m_old - m_new)
    psum = jnp.zeros((MAP_ROWS, PAGE_SIZE), F32)
    pv = jnp.zeros(acc_scr.shape, F32)
    for j in range(pps):
        p = jnp.exp(tiles[j] - m_new)
        psum = psum + p
        pv = pv + _dot(p.astype(BF16), v_refs[j][...].astype(BF16))
    l_scr[...] = alpha * l_scr[...] + jnp.sum(psum, axis=-1, keepdims=True)
    acc_scr[...] = alpha * acc_scr[...] + pv
    m_scr[...] = m_new

    if with_index:
        page_row = lax.broadcasted_iota(I32, (pps, PAGE_SIZE), 0)
        rows = jnp.zeros((pps, PAGE_SIZE), F32)
        for j in range(pps):
            sc = _dot_nt(qi_ref[...], ki_refs[j][...].astype(BF16))
            row = jnp.sum(wi_ref[...] * jnp.maximum(sc, 0.0), axis=0, keepdims=True)
            rows = jnp.where(page_row == j, row, rows)
        score_scr[pl.ds(pl.multiple_of(s * pps, pps), pps), :] = rows

    @pl.when(s == pl.num_programs(1) - 1)
    def _():
        f32 = lambda a: a.astype(F32)
        s_new = jnp.sum(f32(qbd) * f32(knew_ref[...]), axis=-1, keepdims=True)
        if scale != 1.0:
            s_new = s_new * scale
        s_new = s_new + bias_new_ref[:, 0:1]
        if with_index:
            sc_new = jnp.sum(f32(qi_ref[...]) * f32(kinew_ref[...]), axis=-1, keepdims=True)
            score_new = jnp.sum(wi_ref[...] * jnp.maximum(sc_new, 0.0), axis=0, keepdims=True)
            keys = _to_key(score_scr[...])
            key_new = _to_key(score_new)
            pos_idx = (lax.broadcasted_iota(I32, keys.shape, 0) * PAGE_SIZE
                       + lax.broadcasted_iota(I32, keys.shape, 1))
            pos_new = n_pages * PAGE_SIZE
            one = lambda c: jnp.where(c, 1.0, 0.0)
            kf = float(topk)

            def count_ge(cand):
                return _sum_all(one(keys >= cand)) + one(key_new >= cand)

            t0 = jnp.where(count_ge(jnp.int32(0)) >= kf, 0, INT_MIN).astype(I32)

            def bit_step(b, t):
                cand = t + jnp.left_shift(jnp.int32(1), 30 - b)
                return jnp.where(count_ge(cand) >= kf, cand, t)

            thr = lax.fori_loop(0, 31, bit_step, t0)
            need = kf - (_sum_all(one(keys > thr)) + one(key_new > thr))

            def pos_step(b, p):
                cand = p + jnp.left_shift(jnp.int32(1), idx_bits - 1 - b)
                c = (_sum_all(one((keys == thr) & (pos_idx < cand)))
                     + one((key_new == thr) & (pos_new < cand)))
                return jnp.where(c < need, cand, p)

            pos = lax.fori_loop(0, idx_bits, pos_step, jnp.zeros((1, 1), I32))
            sel_ref[...] = one((keys > thr) | ((keys == thr) & (pos_idx <= pos)))
            sel_new = one((key_new > thr) | ((key_new == thr) & (pos_new <= pos)))
            selnew_ref[...] = jnp.broadcast_to(sel_new, selnew_ref.shape)
        else:
            s_new = jnp.where(selnewin_ref[:, 0:1] > 0.5, s_new, NEG)
        m_old2 = m_scr[...]
        m_fin = jnp.maximum(m_old2, s_new)
        alpha2 = jnp.exp(m_old2 - m_fin)
        p_new = jnp.exp(s_new - m_fin)
        l_fin = alpha2 * l_scr[...] + p_new
        acc = alpha2 * acc_scr[...] + f32(p_new.astype(BF16)) * f32(vnew_ref[...])
        o_full = acc / l_fin
        if with_index:
            lam = _lambda_value(lam_ref)
            for hh in range(H_A):
                cols = slice(hh * DV_A, (hh + 1) * DV_A)
                o = o_full[2 * hh:2 * hh + 1, cols] - lam * o_full[2 * hh + 1:2 * hh + 2, cols]
                o = _rms_rows(o, subln_ref[...]) * (1.0 - LAM_INIT)
                o_ref[:, cols] = o.astype(o_ref.dtype)
        else:
            for hh in range(H_B):
                cols = slice(hh * HD_B, (hh + 1) * HD_B)
                o_ref[:, cols] = o_full[hh:hh + 1, cols].astype(o_ref.dtype)


def _sample_bias(table, n_pages, n_rows_per_head):
    past = n_pages * PAGE_SIZE
    dist = past - np.arange(past)
    t = jnp.take(table, jnp.asarray(_t5_bucket_np(dist)), axis=0)
    t = jnp.repeat(t, n_rows_per_head, axis=1)
    pad = MAP_ROWS - t.shape[1]
    t = jnp.pad(t, ((0, 0), (0, pad)))
    tiles = t.reshape(n_pages, PAGE_SIZE, MAP_ROWS).transpose(0, 2, 1)
    new = jnp.pad(jnp.repeat(table[0], n_rows_per_head), (0, pad))
    return tiles.astype(F32), jnp.broadcast_to(new[:, None], (MAP_ROWS, 128)).astype(F32)


def _sample_attention(page_table, q, k_new, v_new, bias, bias_new, cache_k, cache_v, *, group, scale,
                      index_args=None, mask_args=None):
    db, n_pages = page_table.shape
    pps = PAGES_PER_STEP
    assert n_pages % pps == 0
    w = q.shape[-1]
    past = n_pages * PAGE_SIZE
    topk = min(TOPK_MAX, (past + 1) // 4)
    idx_bits = int(math.ceil(math.log2(past + 1)))
    with_index = index_args is not None
    per_b = lambda a: pl.BlockSpec((None,) + a.shape[1:], lambda b, s, pt: (b,) + (0,) * (a.ndim - 1))
    full = lambda a: pl.BlockSpec(a.shape, lambda b, s, pt: (0,) * a.ndim)

    def page_spec(j, width):
        return pl.BlockSpec((None, PAGE_SIZE, width), lambda b, s, pt: (pt[b, s * pps + j], 0, 0))

    bias_spec = pl.BlockSpec((pps, MAP_ROWS, PAGE_SIZE), lambda b, s, pt: (s, 0, 0))
    args = [q, k_new, v_new, bias, bias_new] + [cache_k] * pps + [cache_v] * pps
    specs = ([per_b(q), per_b(k_new), per_b(v_new), bias_spec, full(bias_new)]
             + [page_spec(j, w) for j in range(pps)] * 2)
    o_shape = jax.ShapeDtypeStruct((db, 1, w), BF16)
    o_spec = pl.BlockSpec((None, 1, w), lambda b, s, pt: (b, 0, 0))
    scratch = [pltpu.VMEM((MAP_ROWS, 1), F32), pltpu.VMEM((MAP_ROWS, 1), F32), pltpu.VMEM((MAP_ROWS, w), F32)]
    if with_index:
        qi, wi, ki_new, lam4, subln, cache_i = index_args
        args += [qi, wi, ki_new, lam4, subln] + [cache_i] * pps
        specs += ([per_b(qi), per_b(wi), per_b(ki_new), full(lam4), full(subln)]
                  + [page_spec(j, cache_i.shape[-1]) for j in range(pps)])
        out_shape = [o_shape, jax.ShapeDtypeStruct((db, n_pages, PAGE_SIZE), F32),
                     jax.ShapeDtypeStruct((db, 1, 128), F32)]
        out_specs = [o_spec, pl.BlockSpec((None, n_pages, PAGE_SIZE), lambda b, s, pt: (b, 0, 0)),
                     pl.BlockSpec((None, 1, 128), lambda b, s, pt: (b, 0, 0))]
        scratch.append(pltpu.VMEM((n_pages, PAGE_SIZE), F32))
    else:
        sel, sel_new = mask_args
        args += [sel, sel_new]
        specs += [pl.BlockSpec((None, pps, PAGE_SIZE), lambda b, s, pt: (b, s, 0)), per_b(sel_new)]
        out_shape = [o_shape]
        out_specs = [o_spec]
    kern = functools.partial(_sample_attn_kernel, group=group, scale=scale, with_index=with_index,
                             n_pages=n_pages, topk=topk, idx_bits=idx_bits)
    return pl.pallas_call(
        kern,
        out_shape=out_shape,
        grid_spec=pltpu.PrefetchScalarGridSpec(
            num_scalar_prefetch=1, grid=(db, n_pages // pps),
            in_specs=specs, out_specs=out_specs, scratch_shapes=scratch),
        compiler_params=_cparams(("parallel", "arbitrary")),
        name="sample_diff_index" if with_index else "sample_sparse",
    )(page_table, *args)


def _prep_inproj_weights(w_in_l, q_norm_a, k_norm_a, q_norm_b, k_norm_b):
    c_qi_end = 3 * WA + 3 * WB + WI
    c_misc_end = c_qi_end + D_IDX + H_IDX
    d = w_in_l.shape[0]
    pad = jnp.zeros((d, 128 - D_IDX - H_IDX), w_in_l.dtype)
    w1 = jnp.concatenate([w_in_l[:, :c_misc_end], pad], axis=1).astype(BF16)
    w_gate = w_in_l[:, c_misc_end:].astype(BF16)
    gains = jnp.stack([jnp.tile(q_norm_a, WA // HD_A), jnp.tile(k_norm_a, WA // HD_A),
                       jnp.tile(q_norm_b, H_B), jnp.tile(k_norm_b, H_B)], axis=0).astype(F32)
    grp = np.arange(WA) // HD_A
    g64 = jnp.asarray((grp[:, None] == grp[None, :]).astype(np.float32) / HD_A, dtype=BF16)
    return w1, w_gate, gains, g64


def kernel(x_prompt, x_sample, cache_diff_k, cache_diff_v, cache_dsa_k, cache_dsa_v, cache_idx_k, page_table, rel_bias_table, norm_mix, w_in, q_norm_a, k_norm_a, q_norm_b, k_norm_b, lambda_q1, lambda_k1, lambda_q2, lambda_k2, subln_a, w_proj_a, w_proj_b, w_out, norm_ffn, w_query, sub_keys, expert_u, expert_v):
    assert w_in.shape[0] == 1, "single-layer trunk"
    l = 0
    B, S, D = x_prompt.shape
    DB, DS, _ = x_sample.shape
    assert DS == 1
    n_pool = cache_diff_k.shape[1]
    n_pages = page_table.shape[1]
    bf = lambda a: a.astype(BF16)

    w1, w_gate, gains, g64 = _prep_inproj_weights(w_in[l], q_norm_a[l], k_norm_a[l], q_norm_b[l], k_norm_b[l])
    g_mix = norm_mix[l][None]
    g_ffn = norm_ffn[l][None]
    lam4 = jnp.stack([lambda_q1[l], lambda_k1[l], lambda_q2[l], lambda_k2[l]]).astype(F32)
    subln = subln_a[l][None]
    w_pa, w_pb, w_o, w_q = bf(w_proj_a[l]), bf(w_proj_b[l]), bf(w_out[l]), bf(w_query[l])
    sk = bf(sub_keys[l])
    eu = bf(expert_u[l])
    evt = bf(expert_v[l]).T

    def channel_mix(x2, oa, ob, tt_merge, tt_peer):
        x1, h2, q = _merge(x2, oa, ob, g_mix, w_gate, w_pa, w_pb, w_o, g_ffn, w_q, tt_merge)
        rank1, pb, pn, pa = _peer_select(q, sk)
        return _peer_main(h2, x1, eu, evt, rank1, pb, pn, pa, tt_peer, 1024)

    xp2 = x_prompt.reshape(B * S, D)
    qa, kaf, kab, vaf, vab, qb, kbf, kbb, vbf, vbb, qi, kif, kib, wi = _inproj(xp2, g_mix, w1, gains, g64, 256)
    tq = 256
    bias_p = _prompt_bias_tiles(rel_bias_table, tq)
    r3 = lambda a: a.reshape(B, S, -1)
    oa, ob = _prompt_attention(r3(qa), r3(qb), r3(qi), r3(wi), r3(kab), r3(vab), r3(kbb), r3(vbb), r3(kib),
                               bias_p, lam4, subln, tq)
    y_p = channel_mix(xp2, oa.reshape(B * S, WA), ob.reshape(B * S, WB), 256, 512)

    xs2 = x_sample.reshape(DB, D)
    (qa_s, kaf_s, kab_s, vaf_s, vab_s, qb_s, kbf_s, kbb_s, vbf_s, vbb_s,
     qi_s, kif_s, kib_s, wi_s) = _inproj(xs2, g_mix, w1, gains, g64, DB)
    row = lambda a: a[:, None, :]
    bias_a, bias_a_new = _sample_bias(rel_bias_table[:, :H_A], n_pages, 2)
    oa_s, sel, sel_new = _sample_attention(
        page_table, row(qa_s), row(kab_s), row(vab_s), bias_a, bias_a_new,
        cache_diff_k.reshape(n_pool, PAGE_SIZE, WA), cache_diff_v.reshape(n_pool, PAGE_SIZE, WA),
        group=HD_A, scale=1.0,
        index_args=(qi_s.reshape(DB, H_IDX, D_IDX), wi_s.reshape(DB, H_IDX, 1), row(kib_s), lam4, subln,
                    cache_idx_k.reshape(n_pool, PAGE_SIZE, D_IDX)))
    bias_b, bias_b_new = _sample_bias(rel_bias_table[:, H_A:], n_pages, 1)
    (ob_s,) = _sample_attention(
        page_table, row(qb_s), row(kbb_s), row(vbb_s), bias_b, bias_b_new,
        cache_dsa_k.reshape(n_pool, PAGE_SIZE, WB), cache_dsa_v.reshape(n_pool, PAGE_SIZE, WB),
        group=HD_B, scale=HD_B ** -0.5, mask_args=(sel, sel_new))
    y_s = channel_mix(xs2, oa_s.reshape(DB, WA), ob_s.reshape(DB, WB), DB, DB)

    return (y_p.reshape(B, S, D), y_s.reshape(DB, DS, D),
            kaf.reshape(1, B, S, H_A, 2 * HD_A), vaf.reshape(1, B, S, H_A, DV_A),
            kbf.reshape(1, B, S, H_B, HD_B), vbf.reshape(1, B, S, H_B, HD_B), kif.reshape(1, B, S, D_IDX),
            kaf_s.reshape(1, DB, DS, H_A, 2 * HD_A), vaf_s.reshape(1, DB, DS, H_A, DV_A),
            kbf_s.reshape(1, DB, DS, H_B, HD_B), vbf_s.reshape(1, DB, DS, H_B, HD_B),
            kif_s.reshape(1, DB, DS, D_IDX))
```

```python
import functools
import math

import numpy as np
import jax
import jax.numpy as jnp
from jax import lax
from jax.experimental import pallas as pl
from jax.experimental.pallas import tpu as pltpu

F32 = jnp.float32
BF16 = jnp.bfloat16
I32 = jnp.int32

H_A = 4
HD_A = 64
DV_A = 2 * HD_A
H_B = 4
HD_B = 128
H_IDX = 8
D_IDX = 64
TOPK_MAX = 256
N_BUCKETS = 32
MAX_DISTANCE = 128
PAGE_SIZE = 128
PEER_HEADS = 8
N_KEYS = 128
PEER_DK = 256
PEER_TOPK = 16
EPS = 1e-6
NEG = -1e30
LAM_INIT = 0.8 - 0.6 * math.exp(-0.3 * 0)
INT_MIN = -(2 ** 31)

WA = H_A * 2 * HD_A
WB = H_B * HD_B
WI = H_IDX * D_IDX
N_MAPS_A = 2 * H_A

VMEM_LIMIT = 56 * 1024 * 1024


def _cparams(sem):
    return pltpu.CompilerParams(dimension_semantics=sem, vmem_limit_bytes=VMEM_LIMIT)


def _dot(a, b):
    return jnp.dot(a, b, preferred_element_type=F32)


def _dot_nt(a, b):
    return lax.dot_general(a, b, (((1,), (1,)), ((), ())), preferred_element_type=F32)


def _rms_rows(x, g):
    return x * lax.rsqrt(jnp.mean(x * x, axis=-1, keepdims=True) + EPS) * g


def _t5_bucket_np(dist):
    n = np.maximum(dist, 0)
    max_exact = N_BUCKETS // 2
    nf = np.maximum(n, 1).astype(np.float64)
    large = max_exact + (np.log(nf / max_exact) / math.log(MAX_DISTANCE / max_exact)
                         * (N_BUCKETS - max_exact)).astype(np.int64)
    large = np.minimum(large, N_BUCKETS - 1)
    return np.where(n < max_exact, n, large).astype(np.int32)


def _to_key(x):
    b = pltpu.bitcast(x + 0.0, I32)
    return b ^ ((b >> 31) & 0x7FFFFFFF)


def _inproj_kernel(x_ref, g_ref, w_ref, gn_ref, g64_ref,
                   qa_o, kaf_o, kab_o, vaf_o, vab_o, qb_o, kbf_o, kbb_o, vbf_o, vbb_o,
                   qi_o, kif_o, kib_o, wi_o):
    x = x_ref[...]
    h = _rms_rows(x, g_ref[...]).astype(BF16)

    def proj(c0, n):
        return _dot(h, w_ref[:, c0:c0 + n])

    def rms64(z, gain):
        sq = z * z
        hi = sq.astype(BF16)
        lo = (sq - hi.astype(F32)).astype(BF16)
        ms = _dot(hi, g64_ref[...]) + _dot(lo, g64_ref[...])
        return z * lax.rsqrt(ms + EPS) * gain

    def rms128(z, gain):
        parts = []
        for hh in range(H_B):
            zz = z[:, hh * HD_B:(hh + 1) * HD_B]
            parts.append(zz * lax.rsqrt(jnp.mean(zz * zz, axis=-1, keepdims=True) + EPS))
        return jnp.concatenate(parts, axis=-1) * gain

    qa = rms64(proj(0, WA), gn_ref[0:1, :])
    qa_o[...] = (qa * (HD_A ** -0.5)).astype(BF16)
    ka = rms64(proj(WA, WA), gn_ref[1:2, :])
    kaf_o[...] = ka
    kab_o[...] = ka.astype(BF16)
    va = proj(2 * WA, WA)
    vaf_o[...] = va
    vab_o[...] = va.astype(BF16)
    c = 3 * WA
    qb = rms128(proj(c, WB), gn_ref[2:3, :])
    qb_o[...] = qb.astype(BF16)
    kb = rms128(proj(c + WB, WB), gn_ref[3:4, :])
    kbf_o[...] = kb
    kbb_o[...] = kb.astype(BF16)
    vb = proj(c + 2 * WB, WB)
    vbf_o[...] = vb
    vbb_o[...] = vb.astype(BF16)
    c = c + 3 * WB
    qi = proj(c, WI)
    qi_o[...] = (qi * (D_IDX ** -0.5)).astype(BF16)
    misc = proj(c + WI, 128)
    ki = misc[:, 0:D_IDX]
    kif_o[...] = ki
    kib_o[...] = ki.astype(BF16)
    wi_o[...] = misc[:, D_IDX:D_IDX + H_IDX] * (H_IDX ** -0.5)


def _inproj(x2, g_mix, w1, gains, g64, tt):
    t, d = x2.shape
    assert t % tt == 0
    row = lambda n: pl.BlockSpec((tt, n), lambda i: (i, 0))
    full = lambda a: pl.BlockSpec(a.shape, lambda i: (0,) * a.ndim)
    sd = lambda n, dt: jax.ShapeDtypeStruct((t, n), dt)
    out_shape = [sd(WA, BF16), sd(WA, F32), sd(WA, BF16), sd(WA, F32), sd(WA, BF16),
                 sd(WB, BF16), sd(WB, F32), sd(WB, BF16), sd(WB, F32), sd(WB, BF16),
                 sd(WI, BF16), sd(D_IDX, F32), sd(D_IDX, BF16), sd(H_IDX, F32)]
    out_specs = [row(s.shape[1]) for s in out_shape]
    return pl.pallas_call(
        _inproj_kernel,
        out_shape=out_shape,
        grid=(t // tt,),
        in_specs=[row(d), full(g_mix), full(w1), full(gains), full(g64)],
        out_specs=out_specs,
        compiler_params=_cparams(("parallel",)),
        name="inproj",
    )(x2, g_mix, w1, gains, g64)


def _lambda_value(lam_ref):
    a = jnp.sum(lam_ref[0:1, :] * lam_ref[1:2, :], axis=-1, keepdims=True)
    b = jnp.sum(lam_ref[2:3, :] * lam_ref[3:4, :], axis=-1, keepdims=True)
    return jnp.exp(a) - jnp.exp(b) + LAM_INIT


def _count_rows(pred_f32):
    n = pred_f32.shape[1] // 128
    acc = pred_f32[:, 0:128]
    for i in range(1, n):
        acc = acc + pred_f32[:, i * 128:(i + 1) * 128]
    return acc


def _prompt_attn_kernel(qa_ref, qb_ref, qi_ref, wi_ref, ka_ref, va_ref, kb_ref, vb_ref, ki_ref,
                        bias_ref, lam_ref, subln_ref, oa_ref, ob_ref,
                        keys_scr, sel_scr, m_scr, l_scr, acc_scr, *, tq, topk, idx_bits):
    tk = tq
    qblk = pl.program_id(1)
    nj = qblk + 1
    q0 = qblk * tq
    row_pos = q0 + lax.broadcasted_iota(I32, (tq, tk), 0)
    col_iota = lax.broadcasted_iota(I32, (tq, tk), 1)

    def idx_block(j, carry):
        kij = ki_ref[pl.ds(pl.multiple_of(j * tk, tk), tk), :]
        acc = jnp.zeros((tq, tk), F32)
        for hh in range(H_IDX):
            sc = _dot_nt(qi_ref[:, hh * D_IDX:(hh + 1) * D_IDX], kij)
            acc = acc + wi_ref[:, hh:hh + 1] * jnp.maximum(sc, 0.0)
        adm = (col_iota + j * tk) <= row_pos
        keys_scr[j] = _to_key(jnp.where(adm, acc, NEG))
        return carry

    lax.fori_loop(0, nj, idx_block, 0)

    def count(pred_fn):
        def body(j, acc):
            return acc + _count_rows(pred_fn(keys_scr[j], j))
        acc = lax.fori_loop(0, nj, body, jnp.zeros((tq, 128), F32))
        return jnp.sum(acc, axis=-1, keepdims=True)

    kf = float(topk)
    c0 = count(lambda k, j: jnp.where(k >= 0, 1.0, 0.0))
    t0 = jnp.where(c0 >= kf, 0, INT_MIN).astype(I32)

    def bit_step(b, t):
        cand = t + jnp.left_shift(jnp.int32(1), 30 - b)
        c = count(lambda k, j: jnp.where(k >= cand, 1.0, 0.0))
        return jnp.where(c >= kf, cand, t)

    thr = lax.fori_loop(0, 31, bit_step, t0)

    n_gt = count(lambda k, j: jnp.where(k > thr, 1.0, 0.0))
    need = kf - n_gt

    def pos_step(b, p):
        cand = p + jnp.left_shift(jnp.int32(1), idx_bits - 1 - b)
        c = count(lambda k, j: jnp.where((k == thr) & ((col_iota + j * tk) < cand), 1.0, 0.0))
        return jnp.where(c < need, cand, p)

    pos = lax.fori_loop(0, idx_bits, pos_step, jnp.zeros((tq, 1), I32))

    n_maps = N_MAPS_A + H_B
    m_scr[...] = jnp.full(m_scr.shape, NEG, F32)
    l_scr[...] = jnp.zeros(l_scr.shape, F32)
    acc_scr[...] = jnp.zeros(acc_scr.shape, F32)

    def online_update(mi, s, v):
        m_old = m_scr[mi]
        m_new = jnp.maximum(m_old, jnp.max(s, axis=-1, keepdims=True))
        alpha = jnp.exp(m_old - m_new)
        p = jnp.exp(s - m_new)
        l_scr[mi] = alpha * l_scr[mi] + jnp.sum(p, axis=-1, keepdims=True)
        acc_scr[mi] = alpha * acc_scr[mi] + _dot(p.astype(BF16), v)
        m_scr[mi] = m_new

    def attn_block(j, carry):
        kind = jnp.minimum(qblk - j, 2)
        rows = pl.ds(pl.multiple_of(j * tk, tk), tk)
        for hh in range(H_A):
            v = va_ref[rows, hh * DV_A:(hh + 1) * DV_A]
            bias = bias_ref[hh, kind]
            for c in range(2):
                col = (hh * 2 + c) * HD_A
                s = _dot_nt(qa_ref[:, col:col + HD_A], ka_ref[rows, col:col + HD_A]) + bias
                online_update(hh * 2 + c, s, v)
        key = keys_scr[j]
        sel = (key > thr) | ((key == thr) & ((col_iota + j * tk) <= pos))
        sel_scr[...] = jnp.where(sel, 1.0, 0.0)
        for hh in range(H_B):
            col = hh * HD_B
            s = _dot_nt(qb_ref[:, col:col + HD_B], kb_ref[rows, col:col + HD_B]) * (HD_B ** -0.5)
            s = jnp.where(sel_scr[...] > 0.5, s + bias_ref[H_A + hh, kind], NEG)
            online_update(N_MAPS_A + hh, s, vb_ref[rows, col:col + HD_B])
        return carry

    lax.fori_loop(0, nj, attn_block, 0)

    lam = _lambda_value(lam_ref)
    for hh in range(H_A):
        o = acc_scr[2 * hh] / l_scr[2 * hh] - lam * (acc_scr[2 * hh + 1] / l_scr[2 * hh + 1])
        o = _rms_rows(o, subln_ref[...]) * (1.0 - LAM_INIT)
        oa_ref[:, hh * DV_A:(hh + 1) * DV_A] = o.astype(oa_ref.dtype)
    for hh in range(H_B):
        o = acc_scr[N_MAPS_A + hh] / l_scr[N_MAPS_A + hh]
        ob_ref[:, hh * HD_B:(hh + 1) * HD_B] = o.astype(ob_ref.dtype)


def _prompt_bias_tiles(table, tq):
    n = tq
    m = np.arange(2 * n)
    offs = np.where(m < n, -m, 2 * n - m)
    heads = table.shape[1]
    tiles = []
    for kind in range(3):
        dist = kind * n + offs
        ring = jnp.take(table, jnp.asarray(_t5_bucket_np(dist)), axis=0)
        ring = jnp.where(jnp.asarray(dist >= 0)[:, None], ring, NEG).T
        flat = jnp.tile(ring, (1, n))[:, :n * (2 * n - 1)]
        tiles.append(flat.reshape(heads, n, 2 * n - 1)[:, :, :n])
    return jnp.stack(tiles, axis=1).astype(F32)


def _prompt_attention(qa, qb, qi, wi, ka, va, kb, vb, ki, bias, lam4, subln, tq):
    b, s, _ = qa.shape
    assert s % tq == 0 and tq % 128 == 0
    nq = s // tq
    topk = min(TOPK_MAX, s // 4)
    idx_bits = max(1, int(math.ceil(math.log2(s))))
    qspec = lambda n: pl.BlockSpec((None, tq, n), lambda bi, qi_: (bi, qi_, 0))
    kspec = lambda n: pl.BlockSpec((None, s, n), lambda bi, qi_: (bi, 0, 0))
    full = lambda a: pl.BlockSpec(a.shape, lambda bi, qi_: (0,) * a.ndim)
    n_maps = N_MAPS_A + H_B
    kern = functools.partial(_prompt_attn_kernel, tq=tq, topk=topk, idx_bits=idx_bits)
    return pl.pallas_call(
        kern,
        out_shape=[jax.ShapeDtypeStruct((b, s, WA), BF16), jax.ShapeDtypeStruct((b, s, WB), BF16)],
        grid=(b, nq),
        in_specs=[qspec(WA), qspec(WB), qspec(WI), qspec(H_IDX),
                  kspec(WA), kspec(WA), kspec(WB), kspec(WB), kspec(D_IDX),
                  full(bias), full(lam4), full(subln)],
        out_specs=[qspec(WA), qspec(WB)],
        scratch_shapes=[pltpu.VMEM((nq, tq, tq), I32),
                        pltpu.VMEM((tq, tq), F32),
                        pltpu.VMEM((n_maps, tq, 1), F32),
                        pltpu.VMEM((n_maps, tq, 1), F32),
                        pltpu.VMEM((n_maps, tq, 128), F32)],
        compiler_params=_cparams(("parallel", "arbitrary")),
        name="prompt_attn",
    )(qa, qb, qi, wi, ka, va, kb, vb, ki, bias, lam4, subln)


def _sigmoid(x):
    return 1.0 / (1.0 + jnp.exp(-x))


def _merge_kernel(x_ref, oa_ref, ob_ref, gmix_ref, wg_ref, wpa_ref, wpb_ref, wout_ref, gffn_ref, wq_ref,
                  x1_o, h2_o, q_o):
    x = x_ref[...]
    d = x.shape[1]
    h = _rms_rows(x, gmix_ref[...]).astype(BF16)
    ga = _dot(h, wg_ref[:, 0:d])
    gb = _dot(h, wg_ref[:, d:2 * d])
    m = _sigmoid(ga) * _dot(oa_ref[...], wpa_ref[...]) + _sigmoid(gb) * _dot(ob_ref[...], wpb_ref[...])
    x1 = x + _dot(m.astype(BF16), wout_ref[...])
    x1_o[...] = x1
    h2 = _rms_rows(x1, gffn_ref[...]).astype(BF16)
    h2_o[...] = h2
    q_o[...] = _dot(h2, wq_ref[...]).astype(BF16)


def _merge(x2, oa, ob, g_mix, w_gate, w_pa, w_pb, w_o, g_ffn, w_q, tt):
    t, d = x2.shape
    assert t % tt == 0
    row = lambda n: pl.BlockSpec((tt, n), lambda i: (i, 0))
    full = lambda a: pl.BlockSpec(a.shape, lambda i: (0,) * a.ndim)
    nq = w_q.shape[1]
    return pl.pallas_call(
        _merge_kernel,
        out_shape=[jax.ShapeDtypeStruct((t, d), F32), jax.ShapeDtypeStruct((t, d), BF16),
                   jax.ShapeDtypeStruct((t, nq), BF16)],
        grid=(t // tt,),
        in_specs=[row(d), row(oa.shape[1]), row(ob.shape[1]), full(g_mix), full(w_gate), full(w_pa), full(w_pb),
                  full(w_o), full(g_ffn), full(w_q)],
        out_specs=[row(d), row(d), row(nq)],
        compiler_params=_cparams(("parallel",)),
        name="merge",
    )(x2, oa, ob, g_mix, w_gate, w_pa, w_pb, w_o, g_ffn, w_q)


PEER_STRIP = 128
NOT_SELECTED = float(PEER_TOPK)


def _extract_top(cur, order, n_out):
    lanes = cur.shape[1]
    out_row = lax.broadcasted_iota(I32, (n_out, lanes), 0)
    vals = jnp.zeros((n_out, lanes), F32)
    rank = jnp.full(cur.shape, NOT_SELECTED, F32)
    for r in range(n_out):
        mx = jnp.max(cur, axis=0, keepdims=True)
        first = jnp.min(jnp.where(cur == mx, order, 1e9), axis=0, keepdims=True)
        hit = order == first
        rank = jnp.where(hit, float(r), rank)
        cur = jnp.where(hit, -jnp.inf, cur)
        vals = jnp.where(out_row == r, mx, vals)
    return vals, rank


def _peer_select_kernel(q_ref, sk_ref, rank1_o, b_o, n_o, a_o):
    k = PEER_TOPK
    half = PEER_DK // 2
    lanes = q_ref.shape[0]
    key_idx = lax.broadcasted_iota(I32, (N_KEYS, lanes), 0).astype(F32)
    row8 = lax.broadcasted_iota(I32, (8, lanes), 0).astype(F32)
    row16 = lax.broadcasted_iota(I32, (k, lanes), 0).astype(F32)
    for hh in range(PEER_HEADS):
        c0 = hh * PEER_DK
        s0 = _dot_nt(sk_ref[hh, 0], q_ref[:, c0:c0 + half])
        s1 = _dot_nt(sk_ref[hh, 1], q_ref[:, c0 + half:c0 + PEER_DK])
        t0, rank0 = _extract_top(s0, key_idx, k)
        t1, rank1 = _extract_top(s1, key_idx, k)
        cands = [t0[0:1] + t1]
        orders = [row16]
        for r0 in range(1, 8):
            c = t0[r0:r0 + 1] + t1[0:8]
            lim = k // (r0 + 1)
            if lim < 8:
                c = jnp.where(row8 < float(lim), c, -jnp.inf)
            cands.append(c)
            orders.append(row8 + float(k * r0))
        cands.append(t0[8:16] + t1[0:1])
        orders.append((row8 + 8.0) * float(k))
        cand = jnp.concatenate(cands, axis=0)
        order = jnp.concatenate(orders, axis=0)
        _, crank = _extract_top(cand, order, k)
        chosen = crank < NOT_SELECTED
        z = jnp.sum(jnp.where(chosen, jnp.exp(cand - (t0[0:1] + t1[0:1])), 0.0), axis=0, keepdims=True)
        chosen_f = jnp.where(chosen, 1.0, 0.0)
        n_low = jnp.zeros((8, lanes), F32)
        n_low = jnp.where(row8 == 0.0, jnp.sum(chosen_f[0:16], axis=0, keepdims=True), n_low)
        for r0 in range(1, 8):
            blk = chosen_f[16 + 8 * (r0 - 1):16 + 8 * r0]
            n_low = jnp.where(row8 == float(r0), jnp.sum(blk, axis=0, keepdims=True), n_low)
        n16 = jnp.concatenate([n_low, chosen_f[72:80]], axis=0)
        n_key = jnp.zeros((N_KEYS, lanes), F32)
        for r0 in range(k):
            n_key = jnp.where(rank0 == float(r0), n16[r0:r0 + 1], n_key)
        a = jnp.where(rank0 < NOT_SELECTED, jnp.exp(s0 - t0[0:1]), 0.0)
        b = jnp.where(rank1 < NOT_SELECTED, jnp.exp(s1 - t1[0:1]), 0.0) / z
        rank1_o[hh] = rank1.astype(BF16)
        b_o[hh] = b.astype(BF16)
        n_o[hh] = n_key
        a_o[hh] = a


def _peer_select(q, sk):
    t = q.shape[0]
    assert t % PEER_STRIP == 0
    spec = pl.BlockSpec((PEER_HEADS, N_KEYS, PEER_STRIP), lambda i: (0, 0, i))
    sd = lambda dt: jax.ShapeDtypeStruct((PEER_HEADS, N_KEYS, t), dt)
    return pl.pallas_call(
        _peer_select_kernel,
        out_shape=[sd(BF16), sd(BF16), sd(F32), sd(F32)],
        grid=(t // PEER_STRIP,),
        in_specs=[pl.BlockSpec((PEER_STRIP, q.shape[1]), lambda i: (i, 0)),
                  pl.BlockSpec(sk.shape, lambda i: (0, 0, 0, 0))],
        out_specs=[spec, spec, spec, spec],
        compiler_params=_cparams(("parallel",)),
        name="peer_select",
    )(q, sk)


def _gelu_tanh(x):
    return 0.5 * x * (1.0 + jnp.tanh(math.sqrt(2.0 / math.pi) * (x + 0.044715 * (x * x * x))))


def _peer_main_kernel(h_ref, x1_ref, eu_ref, evt_ref, rank1_ref, b_ref, n_ref, a_ref, y_ref,
                      hact_scr, g_scr, acc_scr, *, rows_per_chunk):
    e = pl.program_id(1)
    tt = h_ref.shape[0]

    @pl.when(e == 0)
    def _():
        acc_scr[...] = jnp.zeros(acc_scr.shape, F32)

    zero = jnp.zeros((), BF16)
    sub = 2 * N_KEYS
    for grp in range(rows_per_chunk // 8):
        base = pl.multiple_of(e * rows_per_chunk + grp * 8, 8)
        for blk in range(8 * N_KEYS // sub):
            r0 = grp * 8 * N_KEYS + blk * sub
            hact_scr[r0:r0 + sub, :] = _dot_nt(eu_ref[r0:r0 + sub, :], h_ref[...])
        for r in range(8):
            for st in range(tt // 128):
                cols = slice(st * 128, (st + 1) * 128)
                w = jnp.zeros((N_KEYS, 128), BF16)
                for hh in range(PEER_HEADS):
                    n_row = n_ref[hh, pl.ds(base, 8), cols][r:r + 1]
                    a_row = a_ref[hh, pl.ds(base, 8), cols][r:r + 1]
                    n_b = jnp.broadcast_to(n_row.astype(BF16), (N_KEYS, 128))
                    a_b = jnp.broadcast_to(a_row.astype(BF16), (N_KEYS, 128))
                    w = w + jnp.where(rank1_ref[hh, :, cols] < n_b, b_ref[hh, :, cols], zero) * a_b
                rows = slice((grp * 8 + r) * N_KEYS, (grp * 8 + r + 1) * N_KEYS)
                g_scr[rows, cols] = _gelu_tanh(hact_scr[rows, cols]).astype(BF16) * w
    acc_scr[...] += _dot(evt_ref[...], g_scr[...])

    @pl.when(e == pl.num_programs(1) - 1)
    def _():
        y_ref[...] = x1_ref[...] + acc_scr[...].T


def _transpose_cast_kernel(x_ref, o_ref):
    o_ref[...] = x_ref[...].T.astype(o_ref.dtype)


def _transpose_cast(x, dtype, blk):
    r, c = x.shape
    assert r % blk == 0
    return pl.pallas_call(
        _transpose_cast_kernel,
        out_shape=jax.ShapeDtypeStruct((c, r), dtype),
        grid=(r // blk,),
        in_specs=[pl.BlockSpec((blk, c), lambda i: (i, 0))],
        out_specs=pl.BlockSpec((c, blk), lambda i: (0, i)),
        compiler_params=_cparams(("parallel",)),
        name="transpose_cast",
    )(x)


def _peer_main(h2, x1, eu, evt, rank1, b, n, a, tt, ec):
    t, d = h2.shape
    n_exp = eu.shape[0]
    assert t % tt == 0 and n_exp % ec == 0 and ec % (8 * N_KEYS) == 0
    tok = lambda: pl.BlockSpec((PEER_HEADS, N_KEYS, tt), lambda i, e: (0, 0, i))
    kern = functools.partial(_peer_main_kernel, rows_per_chunk=ec // N_KEYS)
    return pl.pallas_call(
        kern,
        out_shape=jax.ShapeDtypeStruct((t, d), F32),
        grid=(t // tt, n_exp // ec),
        in_specs=[pl.BlockSpec((tt, d), lambda i, e: (i, 0)),
                  pl.BlockSpec((tt, d), lambda i, e: (i, 0)),
                  pl.BlockSpec((ec, d), lambda i, e: (e, 0)),
                  pl.BlockSpec((d, ec), lambda i, e: (0, e)),
                  tok(), tok(), tok(), tok()],
        out_specs=pl.BlockSpec((tt, d), lambda i, e: (i, 0)),
        scratch_shapes=[pltpu.VMEM((ec, tt), F32), pltpu.VMEM((ec, tt), BF16), pltpu.VMEM((d, tt), F32)],
        compiler_params=_cparams(("parallel", "arbitrary")),
        name="peer_main",
    )(h2, x1, eu, evt, rank1, b, n, a)


PAGES_PER_STEP = 16
MAP_ROWS = 8
HEADS_PER_TOKEN = 4
PAGE_ROWS = PAGE_SIZE * HEADS_PER_TOKEN


def _sum_all(x):
    return jnp.sum(jnp.sum(x, axis=0, keepdims=True), axis=1, keepdims=True)


def _sample_attn_kernel(pt_ref, *refs, scale, with_index):
    pps = PAGES_PER_STEP
    it = iter(refs)
    q_ref = next(it)
    knew_ref = next(it)
    vnew_ref = next(it)
    bias_ref = next(it)
    bias_new_ref = next(it)
    k_refs = [next(it) for _ in range(pps)]
    v_refs = [next(it) for _ in range(pps)]
    if with_index:
        qi_ref = next(it)
        wi_ref = next(it)
        kinew_ref = next(it)
        lam_ref = next(it)
        subln_ref = next(it)
        ki_refs = [next(it) for _ in range(pps)]
        o_ref = next(it)
        score_ref = next(it)
        scorenew_ref = next(it)
        m_scr, l_scr, acc_scr = next(it), next(it), next(it)
    else:
        selin_ref = next(it)
        selnewin_ref = next(it)
        o_ref = next(it)
        m_scr, l_scr, acc_scr = next(it), next(it), next(it)

    s = pl.program_id(1)

    @pl.when(s == 0)
    def _():
        m_scr[...] = jnp.full(m_scr.shape, NEG, F32)
        l_scr[...] = jnp.zeros(l_scr.shape, F32)
        acc_scr[...] = jnp.zeros(acc_scr.shape, F32)

    qbd = q_ref[...]

    tiles = []
    for j in range(pps):
        st = _dot_nt(qbd, k_refs[j][...].astype(BF16))
        if scale != 1.0:
            st = st * scale
        st = st + bias_ref[j]
        if not with_index:
            st = jnp.where(selin_ref[j:j + 1, :] > 0.5, st, NEG)
        tiles.append(st)
    mx = tiles[0]
    for j in range(1, pps):
        mx = jnp.maximum(mx, tiles[j])
    m_old = m_scr[...]
    m_new = jnp.maximum(m_old, jnp.max(mx, axis=-1, keepdims=True))
    alpha = jnp.exp(m_old - m_new)
    psum = jnp.zeros((MAP_ROWS, PAGE_ROWS), F32)
    pv = jnp.zeros(acc_scr.shape, F32)
    for j in range(pps):
        p = jnp.exp(tiles[j] - m_new)
        psum = psum + p
        pv = pv + _dot(p.astype(BF16), v_refs[j][...].astype(BF16))
    l_scr[...] = alpha * l_scr[...] + jnp.sum(psum, axis=-1, keepdims=True)
    acc_scr[...] = alpha * acc_scr[...] + pv
    m_scr[...] = m_new

    if with_index:
        page_row = lax.broadcasted_iota(I32, (pps, PAGE_SIZE), 0)
        rows = jnp.zeros((pps, PAGE_SIZE), F32)
        for j in range(pps):
            sc = _dot_nt(qi_ref[...], ki_refs[j][...].astype(BF16))
            row = jnp.sum(wi_ref[...] * jnp.maximum(sc, 0.0), axis=0, keepdims=True)
            rows = jnp.where(page_row == j, row, rows)
        score_ref[pl.ds(pl.multiple_of(s * pps, pps), pps), :] = rows

    @pl.when(s == pl.num_programs(1) - 1)
    def _():
        f32 = lambda a: a.astype(F32)
        s_new = jnp.sum(f32(qbd) * f32(knew_ref[...]), axis=-1, keepdims=True)
        if scale != 1.0:
            s_new = s_new * scale
        s_new = s_new + bias_new_ref[:, 0:1]
        if with_index:
            sc_new = jnp.sum(f32(qi_ref[...]) * f32(kinew_ref[...]), axis=-1, keepdims=True)
            score_new = jnp.sum(wi_ref[...] * jnp.maximum(sc_new, 0.0), axis=0, keepdims=True)
            scorenew_ref[...] = jnp.broadcast_to(score_new, scorenew_ref.shape)
        else:
            s_new = jnp.where(selnewin_ref[:, 0:1] > 0.5, s_new, NEG)
        m_old2 = m_scr[...]
        m_fin = jnp.maximum(m_old2, s_new)
        alpha2 = jnp.exp(m_old2 - m_fin)
        p_new = jnp.exp(s_new - m_fin)
        l_fin = alpha2 * l_scr[...] + p_new
        acc = alpha2 * acc_scr[...] + f32(p_new.astype(BF16)) * f32(vnew_ref[...])
        o_full = acc / l_fin
        if with_index:
            lam = _lambda_value(lam_ref)
            out_row = lax.broadcasted_iota(I32, o_full.shape, 0)
            out = jnp.zeros(o_full.shape, F32)
            for hh in range(H_A):
                o = o_full[2 * hh:2 * hh + 1] - lam * o_full[2 * hh + 1:2 * hh + 2]
                o = _rms_rows(o, subln_ref[...]) * (1.0 - LAM_INIT)
                out = jnp.where(out_row == hh, o, out)
            o_ref[...] = out
        else:
            o_ref[...] = o_full


def _sample_bias(table, n_pages, n_rows_per_head):
    past = n_pages * PAGE_SIZE
    dist = past - np.arange(past)
    t = jnp.take(table, jnp.asarray(_t5_bucket_np(dist)), axis=0)
    heads = table.shape[1]
    row_head = np.arange(MAP_ROWS) // n_rows_per_head
    own = jnp.asarray(row_head[:, None] == np.arange(HEADS_PER_TOKEN)[None, :])
    t_rows = jnp.take(t, jnp.asarray(np.minimum(row_head, heads - 1)), axis=1)
    tiles = jnp.where(own[None, :, None, :], t_rows.reshape(n_pages, PAGE_SIZE, MAP_ROWS).transpose(0, 2, 1)[..., None], NEG)
    tiles = tiles.reshape(n_pages, MAP_ROWS, PAGE_ROWS)
    new = jnp.where(jnp.asarray(row_head < heads), jnp.take(table[0], jnp.asarray(np.minimum(row_head, heads - 1))), 0.0)
    return tiles.astype(F32), jnp.broadcast_to(new[:, None], (MAP_ROWS, 128)).astype(F32)


def _map_rows(x, n_rows_per_head, split_lanes):
    db = x.shape[0]
    xh = x.reshape(db, HEADS_PER_TOKEN, 1, 128)
    if split_lanes:
        half = (np.arange(128) // (128 // n_rows_per_head))[None, :] == np.arange(n_rows_per_head)[:, None]
        xh = jnp.where(jnp.asarray(half)[None, None], xh, jnp.zeros((), x.dtype))
    else:
        xh = jnp.broadcast_to(xh, (db, HEADS_PER_TOKEN, n_rows_per_head, 128))
    rows = xh.reshape(db, HEADS_PER_TOKEN * n_rows_per_head, 128)
    return jnp.pad(rows, ((0, 0), (0, MAP_ROWS - rows.shape[1]), (0, 0)))


def _sample_attention(page_table, q, k_new, v_new, bias, bias_new, cache_k, cache_v, *, scale,
                      index_args=None, mask_args=None):
    db, n_pages = page_table.shape
    pps = PAGES_PER_STEP
    assert n_pages % pps == 0
    w = 128
    with_index = index_args is not None
    per_b = lambda a: pl.BlockSpec((None,) + a.shape[1:], lambda b, s, pt: (b,) + (0,) * (a.ndim - 1))
    full = lambda a: pl.BlockSpec(a.shape, lambda b, s, pt: (0,) * a.ndim)

    def page_spec(j, rows, width):
        return pl.BlockSpec((None, rows, width), lambda b, s, pt: (pt[b, s * pps + j], 0, 0))

    bias_spec = pl.BlockSpec((pps, MAP_ROWS, PAGE_ROWS), lambda b, s, pt: (s, 0, 0))
    args = [q, k_new, v_new, bias, bias_new] + [cache_k] * pps + [cache_v] * pps
    specs = ([per_b(q), per_b(k_new), per_b(v_new), bias_spec, full(bias_new)]
             + [page_spec(j, PAGE_ROWS, w) for j in range(pps)] * 2)
    o_shape = jax.ShapeDtypeStruct((db, MAP_ROWS, w), F32)
    o_spec = pl.BlockSpec((None, MAP_ROWS, w), lambda b, s, pt: (b, 0, 0))
    scratch = [pltpu.VMEM((MAP_ROWS, 1), F32), pltpu.VMEM((MAP_ROWS, 1), F32), pltpu.VMEM((MAP_ROWS, w), F32)]
    if with_index:
        qi, wi, ki_new, lam4, subln, cache_i = index_args
        args += [qi, wi, ki_new, lam4, subln] + [cache_i] * pps
        specs += ([per_b(qi), per_b(wi), per_b(ki_new), full(lam4), full(subln)]
                  + [page_spec(j, PAGE_SIZE, cache_i.shape[-1]) for j in range(pps)])
        out_shape = [o_shape, jax.ShapeDtypeStruct((db, n_pages, PAGE_SIZE), F32),
                     jax.ShapeDtypeStruct((db, 1, 128), F32)]
        out_specs = [o_spec, pl.BlockSpec((None, n_pages, PAGE_SIZE), lambda b, s, pt: (b, 0, 0)),
                     pl.BlockSpec((None, 1, 128), lambda b, s, pt: (b, 0, 0))]
    else:
        sel, sel_new = mask_args
        args += [sel, sel_new]
        specs += [pl.BlockSpec((None, pps, PAGE_ROWS), lambda b, s, pt: (b, s, 0)), per_b(sel_new)]
        out_shape = [o_shape]
        out_specs = [o_spec]
    kern = functools.partial(_sample_attn_kernel, scale=scale, with_index=with_index)
    return pl.pallas_call(
        kern,
        out_shape=out_shape,
        grid_spec=pltpu.PrefetchScalarGridSpec(
            num_scalar_prefetch=1, grid=(db, n_pages // pps),
            in_specs=specs, out_specs=out_specs, scratch_shapes=scratch),
        compiler_params=_cparams(("parallel", "arbitrary")),
        name="sample_diff_index" if with_index else "sample_sparse",
    )(page_table, *args)


SAMPLE_TOPK_CHUNK = 512


def _sample_topk_kernel(score_ref, scorenew_ref, sel_ref, selnew_ref, keys_scr, *, topk, idx_bits):
    rows, past = score_ref.shape
    ch = SAMPLE_TOPK_CHUNK
    n_ch = past // ch
    keys_scr[...] = _to_key(score_ref[...])
    key_new = _to_key(scorenew_ref[:, 0:1])
    lane = lax.broadcasted_iota(I32, (rows, ch), 1)
    one = lambda c: jnp.where(c, 1.0, 0.0)
    kf = float(topk)

    def count(pred, pred_new):
        acc = jnp.zeros((rows, 128), F32)
        for c in range(n_ch):
            acc = acc + _count_rows(one(pred(keys_scr[:, c * ch:(c + 1) * ch], lane + c * ch)))
        return jnp.sum(acc, axis=-1, keepdims=True) + one(pred_new)

    t0 = jnp.where(count(lambda k, p: k >= 0, key_new >= 0) >= kf, 0, INT_MIN).astype(I32)

    def bit_step(b, t):
        cand = t + jnp.left_shift(jnp.int32(1), 30 - b)
        return jnp.where(count(lambda k, p: k >= cand, key_new >= cand) >= kf, cand, t)

    thr = lax.fori_loop(0, 31, bit_step, t0)
    need = kf - count(lambda k, p: k > thr, key_new > thr)

    def pos_step(b, p_lo):
        cand = p_lo + jnp.left_shift(jnp.int32(1), idx_bits - 1 - b)
        c = count(lambda k, p: (k == thr) & (p < cand), (key_new == thr) & (past < cand))
        return jnp.where(c < need, cand, p_lo)

    pos = lax.fori_loop(0, idx_bits, pos_step, jnp.zeros((rows, 1), I32))
    for c in range(n_ch):
        k = keys_scr[:, c * ch:(c + 1) * ch]
        sel_ref[:, c * ch:(c + 1) * ch] = one((k > thr) | ((k == thr) & ((lane + c * ch) <= pos)))
    sel_new = one((key_new > thr) | ((key_new == thr) & (past <= pos)))
    selnew_ref[...] = jnp.broadcast_to(sel_new, selnew_ref.shape)


def _sample_topk(score, score_new):
    db, past = score.shape
    assert past % SAMPLE_TOPK_CHUNK == 0
    topk = min(TOPK_MAX, (past + 1) // 4)
    idx_bits = int(math.ceil(math.log2(past + 1)))
    kern = functools.partial(_sample_topk_kernel, topk=topk, idx_bits=idx_bits)
    return pl.pallas_call(
        kern,
        out_shape=[jax.ShapeDtypeStruct((db, past), F32), jax.ShapeDtypeStruct((db, 128), F32)],
        scratch_shapes=[pltpu.VMEM((db, past), I32)],
        compiler_params=pltpu.CompilerParams(vmem_limit_bytes=VMEM_LIMIT),
        name="sample_topk",
    )(score, score_new)


def _prep_inproj_weights(w_in_l, q_norm_a, k_norm_a, q_norm_b, k_norm_b):
    c_qi_end = 3 * WA + 3 * WB + WI
    c_misc_end = c_qi_end + D_IDX + H_IDX
    d = w_in_l.shape[0]
    pad = jnp.zeros((d, 128 - D_IDX - H_IDX), w_in_l.dtype)
    w1 = jnp.concatenate([w_in_l[:, :c_misc_end], pad], axis=1).astype(BF16)
    w_gate = w_in_l[:, c_misc_end:].astype(BF16)
    gains = jnp.stack([jnp.tile(q_norm_a, WA // HD_A), jnp.tile(k_norm_a, WA // HD_A),
                       jnp.tile(q_norm_b, H_B), jnp.tile(k_norm_b, H_B)], axis=0).astype(F32)
    grp = np.arange(WA) // HD_A
    g64 = jnp.asarray((grp[:, None] == grp[None, :]).astype(np.float32) / HD_A, dtype=BF16)
    return w1, w_gate, gains, g64


def kernel(x_prompt, x_sample, cache_diff_k, cache_diff_v, cache_dsa_k, cache_dsa_v, cache_idx_k, page_table, rel_bias_table, norm_mix, w_in, q_norm_a, k_norm_a, q_norm_b, k_norm_b, lambda_q1, lambda_k1, lambda_q2, lambda_k2, subln_a, w_proj_a, w_proj_b, w_out, norm_ffn, w_query, sub_keys, expert_u, expert_v):
    assert w_in.shape[0] == 1, "single-layer trunk"
    l = 0
    B, S, D = x_prompt.shape
    DB, DS, _ = x_sample.shape
    assert DS == 1
    n_pool = cache_diff_k.shape[1]
    n_pages = page_table.shape[1]
    bf = lambda a: a.astype(BF16)

    w1, w_gate, gains, g64 = _prep_inproj_weights(w_in[l], q_norm_a[l], k_norm_a[l], q_norm_b[l], k_norm_b[l])
    g_mix = norm_mix[l][None]
    g_ffn = norm_ffn[l][None]
    lam4 = jnp.stack([lambda_q1[l], lambda_k1[l], lambda_q2[l], lambda_k2[l]]).astype(F32)
    subln = subln_a[l][None]
    w_pa, w_pb, w_o, w_q = bf(w_proj_a[l]), bf(w_proj_b[l]), bf(w_out[l]), bf(w_query[l])
    sk = bf(sub_keys[l])
    eu = bf(expert_u[l])
    evt = _transpose_cast(expert_v[l], BF16, 1024)

    def channel_mix(x2, oa, ob, tt_merge, tt_peer):
        x1, h2, q = _merge(x2, oa, ob, g_mix, w_gate, w_pa, w_pb, w_o, g_ffn, w_q, tt_merge)
        rank1, pb, pn, pa = _peer_select(q, sk)
        return _peer_main(h2, x1, eu, evt, rank1, pb, pn, pa, tt_peer, 1024)

    xp2 = x_prompt.reshape(B * S, D)
    qa, kaf, kab, vaf, vab, qb, kbf, kbb, vbf, vbb, qi, kif, kib, wi = _inproj(xp2, g_mix, w1, gains, g64, 256)
    tq = 256
    bias_p = _prompt_bias_tiles(rel_bias_table, tq)
    r3 = lambda a: a.reshape(B, S, -1)
    oa, ob = _prompt_attention(r3(qa), r3(qb), r3(qi), r3(wi), r3(kab), r3(vab), r3(kbb), r3(vbb), r3(kib),
                               bias_p, lam4, subln, tq)
    y_p = channel_mix(xp2, oa.reshape(B * S, WA), ob.reshape(B * S, WB), 256, 512)

    xs2 = x_sample.reshape(DB, D)
    (qa_s, kaf_s, kab_s, vaf_s, vab_s, qb_s, kbf_s, kbb_s, vbf_s, vbb_s,
     qi_s, kif_s, kib_s, wi_s) = _inproj(xs2, g_mix, w1, gains, g64, DB)
    row = lambda a: a[:, None, :]
    page_view = lambda c: c.reshape(n_pool, PAGE_ROWS, 128)
    bias_a, bias_a_new = _sample_bias(rel_bias_table[:, :H_A], n_pages, 2)
    oa_s, score, score_new = _sample_attention(
        page_table, _map_rows(qa_s, 2, True), _map_rows(kab_s, 2, False), _map_rows(vab_s, 2, False),
        bias_a, bias_a_new, page_view(cache_diff_k), page_view(cache_diff_v), scale=1.0,
        index_args=(qi_s.reshape(DB, H_IDX, D_IDX), wi_s.reshape(DB, H_IDX, 1), row(kib_s), lam4, subln,
                    cache_idx_k.reshape(n_pool, PAGE_SIZE, D_IDX)))
    bias_b, bias_b_new = _sample_bias(rel_bias_table[:, H_A:], n_pages, 1)
    sel, sel_new = _sample_topk(score.reshape(DB, n_pages * PAGE_SIZE), score_new.reshape(DB, 128))
    sel_rows = jnp.repeat(sel.reshape(DB, n_pages, PAGE_SIZE), HEADS_PER_TOKEN, axis=-1)
    sel_new = sel_new.reshape(DB, 1, 128)
    (ob_s,) = _sample_attention(
        page_table, _map_rows(qb_s, 1, False), _map_rows(kbb_s, 1, False), _map_rows(vbb_s, 1, False),
        bias_b, bias_b_new, page_view(cache_dsa_k), page_view(cache_dsa_v), scale=HD_B ** -0.5,
        mask_args=(sel_rows, sel_new))
    heads_of = lambda o: o[:, :HEADS_PER_TOKEN, :].reshape(DB, HEADS_PER_TOKEN * 128).astype(BF16)
    y_s = channel_mix(xs2, heads_of(oa_s), heads_of(ob_s), DB, DB)

    return (y_p.reshape(B, S, D), y_s.reshape(DB, DS, D),
            kaf.reshape(1, B, S, H_A, 2 * HD_A), vaf.reshape(1, B, S, H_A, DV_A),
            kbf.reshape(1, B, S, H_B, HD_B), vbf.reshape(1, B, S, H_B, HD_B), kif.reshape(1, B, S, D_IDX),
            kaf_s.reshape(1, DB, DS, H_A, 2 * HD_A), vaf_s.reshape(1, DB, DS, H_A, DV_A),
            kbf_s.reshape(1, DB, DS, H_B, HD_B), vbf_s.reshape(1, DB, DS, H_B, HD_B),
            kif_s.reshape(1, DB, DS, D_IDX))
```

```python
import functools
import math

import numpy as np
import jax
import jax.numpy as jnp
from jax import lax
from jax.experimental import pallas as pl
from jax.experimental.pallas import tpu as pltpu

F32 = jnp.float32
BF16 = jnp.bfloat16
I32 = jnp.int32

H_A = 4
HD_A = 64
DV_A = 2 * HD_A
H_B = 4
HD_B = 128
H_IDX = 8
D_IDX = 64
TOPK_MAX = 256
N_BUCKETS = 32
MAX_DISTANCE = 128
PAGE_SIZE = 128
PEER_HEADS = 8
N_KEYS = 128
PEER_DK = 256
PEER_TOPK = 16
EPS = 1e-6
NEG = -1e30
LAM_INIT = 0.8 - 0.6 * math.exp(-0.3 * 0)
INT_MIN = -(2 ** 31)

WA = H_A * 2 * HD_A
WB = H_B * HD_B
WI = H_IDX * D_IDX
N_MAPS_A = 2 * H_A

VMEM_LIMIT = 56 * 1024 * 1024


def _cparams(sem):
    return pltpu.CompilerParams(dimension_semantics=sem, vmem_limit_bytes=VMEM_LIMIT)


def _dot(a, b):
    return jnp.dot(a, b, preferred_element_type=F32)


def _dot_nt(a, b):
    return lax.dot_general(a, b, (((1,), (1,)), ((), ())), preferred_element_type=F32)


def _rms_rows(x, g):
    return x * lax.rsqrt(jnp.mean(x * x, axis=-1, keepdims=True) + EPS) * g


def _t5_bucket_np(dist):
    n = np.maximum(dist, 0)
    max_exact = N_BUCKETS // 2
    nf = np.maximum(n, 1).astype(np.float64)
    large = max_exact + (np.log(nf / max_exact) / math.log(MAX_DISTANCE / max_exact)
                         * (N_BUCKETS - max_exact)).astype(np.int64)
    large = np.minimum(large, N_BUCKETS - 1)
    return np.where(n < max_exact, n, large).astype(np.int32)


def _to_key(x):
    b = pltpu.bitcast(x + 0.0, I32)
    return b ^ ((b >> 31) & 0x7FFFFFFF)


def _inproj_kernel(x_ref, g_ref, w_ref, gn_ref, g64_ref,
                   qa_o, kaf_o, kab_o, vaf_o, vab_o, qb_o, kbf_o, kbb_o, vbf_o, vbb_o,
                   qi_o, kif_o, kib_o, wi_o, vat_o, vbt_o):
    x = x_ref[...]
    h = _rms_rows(x, g_ref[...]).astype(BF16)

    def proj(c0, n):
        return _dot(h, w_ref[:, c0:c0 + n])

    def rms64(z, gain):
        sq = z * z
        hi = sq.astype(BF16)
        lo = (sq - hi.astype(F32)).astype(BF16)
        ms = _dot(hi, g64_ref[...]) + _dot(lo, g64_ref[...])
        return z * lax.rsqrt(ms + EPS) * gain

    def rms128(z, gain):
        parts = []
        for hh in range(H_B):
            zz = z[:, hh * HD_B:(hh + 1) * HD_B]
            parts.append(zz * lax.rsqrt(jnp.mean(zz * zz, axis=-1, keepdims=True) + EPS))
        return jnp.concatenate(parts, axis=-1) * gain

    qa = rms64(proj(0, WA), gn_ref[0:1, :])
    qa_o[...] = (qa * (HD_A ** -0.5)).astype(BF16)
    ka = rms64(proj(WA, WA), gn_ref[1:2, :])
    kaf_o[...] = ka
    kab_o[...] = ka.astype(BF16)
    va = proj(2 * WA, WA)
    vaf_o[...] = va
    vab_o[...] = va.astype(BF16)
    vat_o[...] = va.T.astype(BF16)
    c = 3 * WA
    qb = rms128(proj(c, WB), gn_ref[2:3, :])
    qb_o[...] = qb.astype(BF16)
    kb = rms128(proj(c + WB, WB), gn_ref[3:4, :])
    kbf_o[...] = kb
    kbb_o[...] = kb.astype(BF16)
    vb = proj(c + 2 * WB, WB)
    vbf_o[...] = vb
    vbb_o[...] = vb.astype(BF16)
    vbt_o[...] = vb.T.astype(BF16)
    c = c + 3 * WB
    qi = proj(c, WI)
    qi_o[...] = (qi * (D_IDX ** -0.5)).astype(BF16)
    misc = proj(c + WI, 128)
    ki = misc[:, 0:D_IDX]
    kif_o[...] = ki
    kib_o[...] = ki.astype(BF16)
    wi_o[...] = misc[:, D_IDX:D_IDX + H_IDX] * (H_IDX ** -0.5)


def _inproj(x2, g_mix, w1, gains, g64, tt):
    t, d = x2.shape
    assert t % tt == 0
    row = lambda n: pl.BlockSpec((tt, n), lambda i: (i, 0))
    full = lambda a: pl.BlockSpec(a.shape, lambda i: (0,) * a.ndim)
    sd = lambda n, dt: jax.ShapeDtypeStruct((t, n), dt)
    out_shape = [sd(WA, BF16), sd(WA, F32), sd(WA, BF16), sd(WA, F32), sd(WA, BF16),
                 sd(WB, BF16), sd(WB, F32), sd(WB, BF16), sd(WB, F32), sd(WB, BF16),
                 sd(WI, BF16), sd(D_IDX, F32), sd(D_IDX, BF16), sd(H_IDX, F32)]
    out_specs = [row(s.shape[1]) for s in out_shape]
    out_shape += [jax.ShapeDtypeStruct((t // tt, WA, tt), BF16), jax.ShapeDtypeStruct((t // tt, WB, tt), BF16)]
    out_specs += [pl.BlockSpec((None, WA, tt), lambda i: (i, 0, 0)), pl.BlockSpec((None, WB, tt), lambda i: (i, 0, 0))]
    return pl.pallas_call(
        _inproj_kernel,
        out_shape=out_shape,
        grid=(t // tt,),
        in_specs=[row(d), full(g_mix), full(w1), full(gains), full(g64)],
        out_specs=out_specs,
        compiler_params=_cparams(("parallel",)),
        name="inproj",
    )(x2, g_mix, w1, gains, g64)


def _lambda_value(lam_ref):
    a = jnp.sum(lam_ref[0:1, :] * lam_ref[1:2, :], axis=-1, keepdims=True)
    b = jnp.sum(lam_ref[2:3, :] * lam_ref[3:4, :], axis=-1, keepdims=True)
    return jnp.exp(a) - jnp.exp(b) + LAM_INIT


def _count_rows(pred_f32):
    n = pred_f32.shape[1] // 128
    acc = pred_f32[:, 0:128]
    for i in range(1, n):
        acc = acc + pred_f32[:, i * 128:(i + 1) * 128]
    return acc


def _fold_rows(x, op):
    acc = x[0:8]
    for i in range(1, x.shape[0] // 8):
        acc = op(acc, x[8 * i:8 * i + 8])
    return acc


def _prompt_attn_kernel(qa_ref, qb_ref, qi_ref, wit_ref, ka_ref, vat_ref, kb_ref, vbt_ref, ki_ref,
                        bias_ref, lam_ref, subln_ref, oa_ref, ob_ref,
                        keys_scr, sel_scr, m_scr, l_scr, acc_scr, alpha_scr, s_scr, p_scr, *, tq, topk, idx_bits):
    tk = tq
    qblk = pl.program_id(1)
    nj = qblk + 1
    q0 = qblk * tq
    q_pos = q0 + lax.broadcasted_iota(I32, (tk, tq), 1)
    key_row = lax.broadcasted_iota(I32, (tk, tq), 0)

    def idx_block(j, carry):
        kij = ki_ref[pl.ds(pl.multiple_of(j * tk, tk), tk), :]
        acc = jnp.zeros((tk, tq), F32)
        for hh in range(H_IDX):
            sc = _dot_nt(kij, qi_ref[:, hh * D_IDX:(hh + 1) * D_IDX])
            acc = acc + wit_ref[hh:hh + 1, :] * jnp.maximum(sc, 0.0)
        adm = (key_row + j * tk) <= q_pos
        keys_scr[j] = _to_key(jnp.where(adm, acc, NEG))
        return carry

    lax.fori_loop(0, nj, idx_block, 0)

    def count(pred_fn):
        def body(j, acc):
            return acc + _fold_rows(pred_fn(keys_scr[j], j), jnp.add)
        acc = lax.fori_loop(0, nj, body, jnp.zeros((8, tq), F32))
        return jnp.sum(acc, axis=0, keepdims=True)

    kf = float(topk)
    c0 = count(lambda k, j: jnp.where(k >= 0, 1.0, 0.0))
    t0 = jnp.where(c0 >= kf, 0, INT_MIN).astype(I32)

    def bit_step(b, t):
        cand = t + jnp.left_shift(jnp.int32(1), 30 - b)
        c = count(lambda k, j: jnp.where(k >= cand, 1.0, 0.0))
        return jnp.where(c >= kf, cand, t)

    thr = lax.fori_loop(0, 31, bit_step, t0)

    n_gt = count(lambda k, j: jnp.where(k > thr, 1.0, 0.0))
    need = kf - n_gt

    def pos_step(b, p):
        cand = p + jnp.left_shift(jnp.int32(1), idx_bits - 1 - b)
        c = count(lambda k, j: jnp.where((k == thr) & ((key_row + j * tk) < cand), 1.0, 0.0))
        return jnp.where(c < need, cand, p)

    pos = lax.fori_loop(0, idx_bits, pos_step, jnp.zeros((1, tq), I32))

    m_scr[...] = jnp.full(m_scr.shape, NEG, F32)
    l_scr[...] = jnp.zeros(l_scr.shape, F32)
    acc_scr[...] = jnp.zeros(acc_scr.shape, F32)

    n_maps = N_MAPS_A + H_B

    def attn_block(j, carry):
        kind = jnp.minimum(qblk - j, 2)
        rows = pl.ds(pl.multiple_of(j * tk, tk), tk)
        for hh in range(H_A):
            bias = bias_ref[hh, kind]
            for c in range(2):
                col = (hh * 2 + c) * HD_A
                s_scr[hh * 2 + c] = _dot_nt(ka_ref[rows, col:col + HD_A], qa_ref[:, col:col + HD_A]) + bias
        key = keys_scr[j]
        sel = (key > thr) | ((key == thr) & ((key_row + j * tk) <= pos))
        sel_scr[...] = jnp.where(sel, 1.0, 0.0)
        for hh in range(H_B):
            col = hh * HD_B
            s = _dot_nt(kb_ref[rows, col:col + HD_B], qb_ref[:, col:col + HD_B]) * (HD_B ** -0.5)
            s_scr[N_MAPS_A + hh] = jnp.where(sel_scr[...] > 0.5, s + bias_ref[H_A + hh, kind], NEG)
        for mi in range(n_maps):
            s = s_scr[mi]
            m_old = m_scr[mi]
            m_new = jnp.maximum(m_old, jnp.max(_fold_rows(s, jnp.maximum), axis=0, keepdims=True))
            alpha = jnp.exp(m_old - m_new)
            p = jnp.exp(s - m_new)
            l_scr[mi] = alpha * l_scr[mi] + jnp.sum(_fold_rows(p, jnp.add), axis=0, keepdims=True)
            m_scr[mi] = m_new
            alpha_scr[mi] = alpha
            p_scr[mi] = p.astype(BF16)
        for mi in range(n_maps):
            if mi < N_MAPS_A:
                v_t = vat_ref[j, (mi // 2) * DV_A:(mi // 2 + 1) * DV_A, :]
            else:
                v_t = vbt_ref[j, (mi - N_MAPS_A) * HD_B:(mi - N_MAPS_A + 1) * HD_B, :]
            acc_scr[mi] = alpha_scr[mi] * acc_scr[mi] + _dot(v_t, p_scr[mi])
        return carry

    lax.fori_loop(0, nj, attn_block, 0)

    lam = _lambda_value(lam_ref)
    for hh in range(H_A):
        o_t = acc_scr[2 * hh] / l_scr[2 * hh] - lam * (acc_scr[2 * hh + 1] / l_scr[2 * hh + 1])
        ms = jnp.sum(_fold_rows(o_t * o_t, jnp.add), axis=0, keepdims=True) * (1.0 / DV_A)
        o = (o_t * lax.rsqrt(ms + EPS)).T * subln_ref[...] * (1.0 - LAM_INIT)
        oa_ref[:, hh * DV_A:(hh + 1) * DV_A] = o.astype(oa_ref.dtype)
    for hh in range(H_B):
        o_t = acc_scr[N_MAPS_A + hh] / l_scr[N_MAPS_A + hh]
        ob_ref[:, hh * HD_B:(hh + 1) * HD_B] = o_t.T.astype(ob_ref.dtype)


def _prompt_bias_tiles(table, tq):
    n = tq
    m = np.arange(2 * n)
    offs = np.where(m < n, -m, 2 * n - m)
    heads = table.shape[1]
    tiles = []
    for kind in range(3):
        dist = kind * n + offs
        ring = jnp.take(table, jnp.asarray(_t5_bucket_np(dist)), axis=0)
        ring = jnp.where(jnp.asarray(dist >= 0)[:, None], ring, NEG).T
        flat = jnp.tile(ring, (1, n))[:, :n * (2 * n - 1)]
        tiles.append(flat.reshape(heads, n, 2 * n - 1)[:, :, :n])
    return jnp.swapaxes(jnp.stack(tiles, axis=1), -1, -2).astype(F32)


def _prompt_attention(qa, qb, qi, wi_t, ka, va_t, kb, vb_t, ki, bias, lam4, subln, tq):
    b, s, _ = qa.shape
    assert s % tq == 0 and tq % 128 == 0
    nq = s // tq
    assert va_t.shape == (b * nq, WA, tq) and vb_t.shape == (b * nq, WB, tq)
    topk = min(TOPK_MAX, s // 4)
    idx_bits = max(1, int(math.ceil(math.log2(s))))
    qspec = lambda n: pl.BlockSpec((None, tq, n), lambda bi, qi_: (bi, qi_, 0))
    kspec = lambda n: pl.BlockSpec((None, s, n), lambda bi, qi_: (bi, 0, 0))
    vspec = lambda n: pl.BlockSpec((nq, n, tq), lambda bi, qi_: (bi, 0, 0))
    full = lambda a: pl.BlockSpec(a.shape, lambda bi, qi_: (0,) * a.ndim)
    n_maps = N_MAPS_A + H_B
    kern = functools.partial(_prompt_attn_kernel, tq=tq, topk=topk, idx_bits=idx_bits)
    return pl.pallas_call(
        kern,
        out_shape=[jax.ShapeDtypeStruct((b, s, WA), BF16), jax.ShapeDtypeStruct((b, s, WB), BF16)],
        grid=(b, nq),
        in_specs=[qspec(WA), qspec(WB), qspec(WI),
                  pl.BlockSpec((None, H_IDX, tq), lambda bi, qi_: (bi, 0, qi_)),
                  kspec(WA), vspec(WA), kspec(WB), vspec(WB), kspec(D_IDX),
                  full(bias), full(lam4), full(subln)],
        out_specs=[qspec(WA), qspec(WB)],
        scratch_shapes=[pltpu.VMEM((nq, tq, tq), I32),
                        pltpu.VMEM((tq, tq), F32),
                        pltpu.VMEM((n_maps, 1, tq), F32),
                        pltpu.VMEM((n_maps, 1, tq), F32),
                        pltpu.VMEM((n_maps, 128, tq), F32),
                        pltpu.VMEM((n_maps, 1, tq), F32),
                        pltpu.VMEM((n_maps, tq, tq), F32),
                        pltpu.VMEM((n_maps, tq, tq), BF16)],
        compiler_params=_cparams(("parallel", "arbitrary")),
        name="prompt_attn",
    )(qa, qb, qi, wi_t, ka, va_t, kb, vb_t, ki, bias, lam4, subln)


def _sigmoid(x):
    return 1.0 / (1.0 + jnp.exp(-x))


def _merge_kernel(x_ref, oa_ref, ob_ref, gmix_ref, wg_ref, wpa_ref, wpb_ref, wout_ref, gffn_ref, wq_ref,
                  x1_o, h2_o, q_o):
    x = x_ref[...]
    d = x.shape[1]
    h = _rms_rows(x, gmix_ref[...]).astype(BF16)
    ga = _dot(h, wg_ref[:, 0:d])
    gb = _dot(h, wg_ref[:, d:2 * d])
    m = _sigmoid(ga) * _dot(oa_ref[...], wpa_ref[...]) + _sigmoid(gb) * _dot(ob_ref[...], wpb_ref[...])
    x1 = x + _dot(m.astype(BF16), wout_ref[...])
    x1_o[...] = x1
    h2 = _rms_rows(x1, gffn_ref[...]).astype(BF16)
    h2_o[...] = h2
    q_o[...] = _dot(h2, wq_ref[...]).astype(BF16)


def _merge(x2, oa, ob, g_mix, w_gate, w_pa, w_pb, w_o, g_ffn, w_q, tt):
    t, d = x2.shape
    assert t % tt == 0
    row = lambda n: pl.BlockSpec((tt, n), lambda i: (i, 0))
    full = lambda a: pl.BlockSpec(a.shape, lambda i: (0,) * a.ndim)
    nq = w_q.shape[1]
    return pl.pallas_call(
        _merge_kernel,
        out_shape=[jax.ShapeDtypeStruct((t, d), F32), jax.ShapeDtypeStruct((t, d), BF16),
                   jax.ShapeDtypeStruct((t, nq), BF16)],
        grid=(t // tt,),
        in_specs=[row(d), row(oa.shape[1]), row(ob.shape[1]), full(g_mix), full(w_gate), full(w_pa), full(w_pb),
                  full(w_o), full(g_ffn), full(w_q)],
        out_specs=[row(d), row(d), row(nq)],
        compiler_params=_cparams(("parallel",)),
        name="merge",
    )(x2, oa, ob, g_mix, w_gate, w_pa, w_pb, w_o, g_ffn, w_q)


PEER_STRIP = 128
NOT_SELECTED = float(PEER_TOPK)


def _extract_top(cur, order, n_out):
    lanes = cur.shape[1]
    out_row = lax.broadcasted_iota(I32, (n_out, lanes), 0)
    vals = jnp.zeros((n_out, lanes), F32)
    rank = jnp.full(cur.shape, NOT_SELECTED, F32)
    for r in range(n_out):
        mx = jnp.max(cur, axis=0, keepdims=True)
        first = jnp.min(jnp.where(cur == mx, order, 1e9), axis=0, keepdims=True)
        hit = order == first
        rank = jnp.where(hit, float(r), rank)
        cur = jnp.where(hit, -jnp.inf, cur)
        vals = jnp.where(out_row == r, mx, vals)
    return vals, rank


def _peer_select_kernel(q_ref, sk_ref, rank1_o, b_o, n_o, a_o):
    k = PEER_TOPK
    half = PEER_DK // 2
    lanes = q_ref.shape[0]
    key_idx = lax.broadcasted_iota(I32, (N_KEYS, lanes), 0).astype(F32)
    row8 = lax.broadcasted_iota(I32, (8, lanes), 0).astype(F32)
    row16 = lax.broadcasted_iota(I32, (k, lanes), 0).astype(F32)
    for hh in range(PEER_HEADS):
        c0 = hh * PEER_DK
        s0 = _dot_nt(sk_ref[hh, 0], q_ref[:, c0:c0 + half])
        s1 = _dot_nt(sk_ref[hh, 1], q_ref[:, c0 + half:c0 + PEER_DK])
        t0, rank0 = _extract_top(s0, key_idx, k)
        t1, rank1 = _extract_top(s1, key_idx, k)
        cands = [t0[0:1] + t1]
        orders = [row16]
        for r0 in range(1, 8):
            c = t0[r0:r0 + 1] + t1[0:8]
            lim = k // (r0 + 1)
            if lim < 8:
                c = jnp.where(row8 < float(lim), c, -jnp.inf)
            cands.append(c)
            orders.append(row8 + float(k * r0))
        cands.append(t0[8:16] + t1[0:1])
        orders.append((row8 + 8.0) * float(k))
        cand = jnp.concatenate(cands, axis=0)
        order = jnp.concatenate(orders, axis=0)
        _, crank = _extract_top(cand, order, k)
        chosen = crank < NOT_SELECTED
        z = jnp.sum(jnp.where(chosen, jnp.exp(cand - (t0[0:1] + t1[0:1])), 0.0), axis=0, keepdims=True)
        chosen_f = jnp.where(chosen, 1.0, 0.0)
        n_low = jnp.zeros((8, lanes), F32)
        n_low = jnp.where(row8 == 0.0, jnp.sum(chosen_f[0:16], axis=0, keepdims=True), n_low)
        for r0 in range(1, 8):
            blk = chosen_f[16 + 8 * (r0 - 1):16 + 8 * r0]
            n_low = jnp.where(row8 == float(r0), jnp.sum(blk, axis=0, keepdims=True), n_low)
        n16 = jnp.concatenate([n_low, chosen_f[72:80]], axis=0)
        n_key = jnp.zeros((N_KEYS, lanes), F32)
        for r0 in range(k):
            n_key = jnp.where(rank0 == float(r0), n16[r0:r0 + 1], n_key)
        a = jnp.where(rank0 < NOT_SELECTED, jnp.exp(s0 - t0[0:1]), 0.0)
        b = jnp.where(rank1 < NOT_SELECTED, jnp.exp(s1 - t1[0:1]), 0.0) / z
        rank1_o[hh] = rank1.astype(BF16)
        b_o[hh] = b.astype(BF16)
        n_o[hh] = n_key
        a_o[hh] = a


def _peer_select(q, sk):
    t = q.shape[0]
    assert t % PEER_STRIP == 0
    spec = pl.BlockSpec((PEER_HEADS, N_KEYS, PEER_STRIP), lambda i: (0, 0, i))
    sd = lambda dt: jax.ShapeDtypeStruct((PEER_HEADS, N_KEYS, t), dt)
    return pl.pallas_call(
        _peer_select_kernel,
        out_shape=[sd(BF16), sd(BF16), sd(F32), sd(F32)],
        grid=(t // PEER_STRIP,),
        in_specs=[pl.BlockSpec((PEER_STRIP, q.shape[1]), lambda i: (i, 0)),
                  pl.BlockSpec(sk.shape, lambda i: (0, 0, 0, 0))],
        out_specs=[spec, spec, spec, spec],
        compiler_params=_cparams(("parallel",)),
        name="peer_select",
    )(q, sk)


def _gelu_tanh(x):
    return 0.5 * x * (1.0 + jnp.tanh(math.sqrt(2.0 / math.pi) * (x + 0.044715 * (x * x * x))))


def _peer_main_kernel(h_ref, x1_ref, eu_ref, evt_ref, rank1_ref, b_ref, n_ref, a_ref, y_ref,
                      hact_scr, g_scr, acc_scr, *, rows_per_chunk):
    e = pl.program_id(1)
    tt = h_ref.shape[0]

    @pl.when(e == 0)
    def _():
        acc_scr[...] = jnp.zeros(acc_scr.shape, F32)

    zero = jnp.zeros((), BF16)
    sub = 2 * N_KEYS
    for grp in range(rows_per_chunk // 8):
        base = pl.multiple_of(e * rows_per_chunk + grp * 8, 8)
        for blk in range(8 * N_KEYS // sub):
            r0 = grp * 8 * N_KEYS + blk * sub
            hact_scr[r0:r0 + sub, :] = _dot_nt(eu_ref[r0:r0 + sub, :], h_ref[...])
        for r in range(8):
            for st in range(tt // 128):
                cols = slice(st * 128, (st + 1) * 128)
                w = jnp.zeros((N_KEYS, 128), BF16)
                for hh in range(PEER_HEADS):
                    n_row = n_ref[hh, pl.ds(base, 8), cols][r:r + 1]
                    a_row = a_ref[hh, pl.ds(base, 8), cols][r:r + 1]
                    n_b = jnp.broadcast_to(n_row.astype(BF16), (N_KEYS, 128))
                    a_b = jnp.broadcast_to(a_row.astype(BF16), (N_KEYS, 128))
                    w = w + jnp.where(rank1_ref[hh, :, cols] < n_b, b_ref[hh, :, cols], zero) * a_b
                rows = slice((grp * 8 + r) * N_KEYS, (grp * 8 + r + 1) * N_KEYS)
                g_scr[rows, cols] = _gelu_tanh(hact_scr[rows, cols]).astype(BF16) * w
    acc_scr[...] += _dot(evt_ref[...], g_scr[...])

    @pl.when(e == pl.num_programs(1) - 1)
    def _():
        y_ref[...] = x1_ref[...] + acc_scr[...].T


def _transpose_cast_kernel(x_ref, o_ref):
    o_ref[...] = x_ref[...].T.astype(o_ref.dtype)


def _transpose_cast(x, dtype, blk):
    r, c = x.shape
    assert r % blk == 0
    return pl.pallas_call(
        _transpose_cast_kernel,
        out_shape=jax.ShapeDtypeStruct((c, r), dtype),
        grid=(r // blk,),
        in_specs=[pl.BlockSpec((blk, c), lambda i: (i, 0))],
        out_specs=pl.BlockSpec((c, blk), lambda i: (0, i)),
        compiler_params=_cparams(("parallel",)),
        name="transpose_cast",
    )(x)


def _peer_main(h2, x1, eu, evt, rank1, b, n, a, tt, ec):
    t, d = h2.shape
    n_exp = eu.shape[0]
    assert t % tt == 0 and n_exp % ec == 0 and ec % (8 * N_KEYS) == 0
    tok = lambda: pl.BlockSpec((PEER_HEADS, N_KEYS, tt), lambda i, e: (0, 0, i))
    kern = functools.partial(_peer_main_kernel, rows_per_chunk=ec // N_KEYS)
    return pl.pallas_call(
        kern,
        out_shape=jax.ShapeDtypeStruct((t, d), F32),
        grid=(t // tt, n_exp // ec),
        in_specs=[pl.BlockSpec((tt, d), lambda i, e: (i, 0)),
                  pl.BlockSpec((tt, d), lambda i, e: (i, 0)),
                  pl.BlockSpec((ec, d), lambda i, e: (e, 0)),
                  pl.BlockSpec((d, ec), lambda i, e: (0, e)),
                  tok(), tok(), tok(), tok()],
        out_specs=pl.BlockSpec((tt, d), lambda i, e: (i, 0)),
        scratch_shapes=[pltpu.VMEM((ec, tt), F32), pltpu.VMEM((ec, tt), BF16), pltpu.VMEM((d, tt), F32)],
        compiler_params=_cparams(("parallel", "arbitrary")),
        name="peer_main",
    )(h2, x1, eu, evt, rank1, b, n, a)


PAGES_PER_STEP = 16
MAP_ROWS = 8
HEADS_PER_TOKEN = 4
PAGE_ROWS = PAGE_SIZE * HEADS_PER_TOKEN


def _sum_all(x):
    return jnp.sum(jnp.sum(x, axis=0, keepdims=True), axis=1, keepdims=True)


def _sample_attn_kernel(pt_ref, *refs, scale, with_index):
    pps = PAGES_PER_STEP
    it = iter(refs)
    q_ref = next(it)
    knew_ref = next(it)
    vnew_ref = next(it)
    bias_ref = next(it)
    bias_new_ref = next(it)
    k_refs = [next(it) for _ in range(pps)]
    v_refs = [next(it) for _ in range(pps)]
    if with_index:
        qi_ref = next(it)
        wi_ref = next(it)
        kinew_ref = next(it)
        lam_ref = next(it)
        subln_ref = next(it)
        ki_refs = [next(it) for _ in range(pps)]
        o_ref = next(it)
        score_ref = next(it)
        scorenew_ref = next(it)
        m_scr, l_scr, acc_scr = next(it), next(it), next(it)
    else:
        selin_ref = next(it)
        selnewin_ref = next(it)
        o_ref = next(it)
        m_scr, l_scr, acc_scr = next(it), next(it), next(it)

    s = pl.program_id(1)

    @pl.when(s == 0)
    def _():
        m_scr[...] = jnp.full(m_scr.shape, NEG, F32)
        l_scr[...] = jnp.zeros(l_scr.shape, F32)
        acc_scr[...] = jnp.zeros(acc_scr.shape, F32)

    qbd = q_ref[...]

    tiles = []
    for j in range(pps):
        st = _dot_nt(qbd, k_refs[j][...].astype(BF16))
        if scale != 1.0:
            st = st * scale
        st = st + bias_ref[j]
        if not with_index:
            st = jnp.where(selin_ref[j:j + 1, :] > 0.5, st, NEG)
        tiles.append(st)
    mx = tiles[0]
    for j in range(1, pps):
        mx = jnp.maximum(mx, tiles[j])
    m_old = m_scr[...]
    m_new = jnp.maximum(m_old, jnp.max(mx, axis=-1, keepdims=True))
    alpha = jnp.exp(m_old - m_new)
    psum = jnp.zeros((MAP_ROWS, PAGE_ROWS), F32)
    pv = jnp.zeros(acc_scr.shape, F32)
    for j in range(pps):
        p = jnp.exp(tiles[j] - m_new)
        psum = psum + p
        pv = pv + _dot(p.astype(BF16), v_refs[j][...].astype(BF16))
    l_scr[...] = alpha * l_scr[...] + jnp.sum(psum, axis=-1, keepdims=True)
    acc_scr[...] = alpha * acc_scr[...] + pv
    m_scr[...] = m_new

    if with_index:
        page_row = lax.broadcasted_iota(I32, (pps, PAGE_SIZE), 0)
        rows = jnp.zeros((pps, PAGE_SIZE), F32)
        for j in range(pps):
            sc = _dot(qi_ref[...], ki_refs[j][...].astype(BF16))
            row = jnp.sum(wi_ref[...] * jnp.maximum(sc, 0.0), axis=0, keepdims=True)
            rows = jnp.where(page_row == j, row, rows)
        score_ref[pl.ds(pl.multiple_of(s * pps, pps), pps), :] = rows

    @pl.when(s == pl.num_programs(1) - 1)
    def _():
        f32 = lambda a: a.astype(F32)
        s_new = jnp.sum(f32(qbd) * f32(knew_ref[...]), axis=-1, keepdims=True)
        if scale != 1.0:
            s_new = s_new * scale
        s_new = s_new + bias_new_ref[:, 0:1]
        if with_index:
            sc_new = jnp.sum(f32(qi_ref[...]) * f32(kinew_ref[...]), axis=-1, keepdims=True)
            score_new = jnp.sum(wi_ref[...] * jnp.maximum(sc_new, 0.0), axis=0, keepdims=True)
            scorenew_ref[...] = jnp.broadcast_to(score_new, scorenew_ref.shape)
        else:
            s_new = jnp.where(selnewin_ref[:, 0:1] > 0.5, s_new, NEG)
        m_old2 = m_scr[...]
        m_fin = jnp.maximum(m_old2, s_new)
        alpha2 = jnp.exp(m_old2 - m_fin)
        p_new = jnp.exp(s_new - m_fin)
        l_fin = alpha2 * l_scr[...] + p_new
        acc = alpha2 * acc_scr[...] + f32(p_new.astype(BF16)) * f32(vnew_ref[...])
        o_full = acc / l_fin
        if with_index:
            lam = _lambda_value(lam_ref)
            out_row = lax.broadcasted_iota(I32, o_full.shape, 0)
            out = jnp.zeros(o_full.shape, F32)
            for hh in range(H_A):
                o = o_full[2 * hh:2 * hh + 1] - lam * o_full[2 * hh + 1:2 * hh + 2]
                o = _rms_rows(o, subln_ref[...]) * (1.0 - LAM_INIT)
                out = jnp.where(out_row == hh, o, out)
            o_ref[...] = out
        else:
            o_ref[...] = o_full


def _sample_bias(table, n_pages, n_rows_per_head):
    past = n_pages * PAGE_SIZE
    dist = past - np.arange(past)
    t = jnp.take(table, jnp.asarray(_t5_bucket_np(dist)), axis=0)
    heads = table.shape[1]
    row_head = np.arange(MAP_ROWS) // n_rows_per_head
    own = jnp.asarray(row_head[:, None] == np.arange(HEADS_PER_TOKEN)[None, :])
    t_rows = jnp.take(t, jnp.asarray(np.minimum(row_head, heads - 1)), axis=1)
    tiles = jnp.where(own[None, :, None, :], t_rows.reshape(n_pages, PAGE_SIZE, MAP_ROWS).transpose(0, 2, 1)[..., None], NEG)
    tiles = tiles.reshape(n_pages, MAP_ROWS, PAGE_ROWS)
    new = jnp.where(jnp.asarray(row_head < heads), jnp.take(table[0], jnp.asarray(np.minimum(row_head, heads - 1))), 0.0)
    return tiles.astype(F32), jnp.broadcast_to(new[:, None], (MAP_ROWS, 128)).astype(F32)


def _map_rows(x, n_rows_per_head, split_lanes):
    db = x.shape[0]
    xh = x.reshape(db, HEADS_PER_TOKEN, 1, 128)
    if split_lanes:
        half = (np.arange(128) // (128 // n_rows_per_head))[None, :] == np.arange(n_rows_per_head)[:, None]
        xh = jnp.where(jnp.asarray(half)[None, None], xh, jnp.zeros((), x.dtype))
    else:
        xh = jnp.broadcast_to(xh, (db, HEADS_PER_TOKEN, n_rows_per_head, 128))
    rows = xh.reshape(db, HEADS_PER_TOKEN * n_rows_per_head, 128)
    return jnp.pad(rows, ((0, 0), (0, MAP_ROWS - rows.shape[1]), (0, 0)))


def _sample_attention(page_table, q, k_new, v_new, bias, bias_new, cache_k, cache_v, *, scale,
                      index_args=None, mask_args=None):
    db, n_pages = page_table.shape
    pps = PAGES_PER_STEP
    assert n_pages % pps == 0
    w = 128
    with_index = index_args is not None
    per_b = lambda a: pl.BlockSpec((None,) + a.shape[1:], lambda b, s, pt: (b,) + (0,) * (a.ndim - 1))
    full = lambda a: pl.BlockSpec(a.shape, lambda b, s, pt: (0,) * a.ndim)

    def page_spec(j, rows, width):
        return pl.BlockSpec((None, rows, width), lambda b, s, pt: (pt[b, s * pps + j], 0, 0))

    bias_spec = pl.BlockSpec((pps, MAP_ROWS, PAGE_ROWS), lambda b, s, pt: (s, 0, 0))
    args = [q, k_new, v_new, bias, bias_new] + [cache_k] * pps + [cache_v] * pps
    specs = ([per_b(q), per_b(k_new), per_b(v_new), bias_spec, full(bias_new)]
             + [page_spec(j, PAGE_ROWS, w) for j in range(pps)] * 2)
    o_shape = jax.ShapeDtypeStruct((db, MAP_ROWS, w), F32)
    o_spec = pl.BlockSpec((None, MAP_ROWS, w), lambda b, s, pt: (b, 0, 0))
    scratch = [pltpu.VMEM((MAP_ROWS, 1), F32), pltpu.VMEM((MAP_ROWS, 1), F32), pltpu.VMEM((MAP_ROWS, w), F32)]
    if with_index:
        qi, wi, ki_new, lam4, subln, cache_i = index_args
        args += [qi, wi, ki_new, lam4, subln] + [cache_i] * pps
        specs += ([per_b(qi), per_b(wi), per_b(ki_new), full(lam4), full(subln)]
                  + [page_spec(j, cache_i.shape[-2], PAGE_SIZE) for j in range(pps)])
        out_shape = [o_shape, jax.ShapeDtypeStruct((db, n_pages, PAGE_SIZE), F32),
                     jax.ShapeDtypeStruct((db, 1, 128), F32)]
        out_specs = [o_spec, pl.BlockSpec((None, n_pages, PAGE_SIZE), lambda b, s, pt: (b, 0, 0)),
                     pl.BlockSpec((None, 1, 128), lambda b, s, pt: (b, 0, 0))]
    else:
        sel, sel_new = mask_args
        args += [sel, sel_new]
        specs += [pl.BlockSpec((None, pps, PAGE_ROWS), lambda b, s, pt: (b, s, 0)), per_b(sel_new)]
        out_shape = [o_shape]
        out_specs = [o_spec]
    kern = functools.partial(_sample_attn_kernel, scale=scale, with_index=with_index)
    return pl.pallas_call(
        kern,
        out_shape=out_shape,
        grid_spec=pltpu.PrefetchScalarGridSpec(
            num_scalar_prefetch=1, grid=(db, n_pages // pps),
            in_specs=specs, out_specs=out_specs, scratch_shapes=scratch),
        compiler_params=_cparams(("parallel", "arbitrary")),
        name="sample_diff_index" if with_index else "sample_sparse",
    )(page_table, *args)


SAMPLE_TOPK_CHUNK = 512


def _sample_topk_kernel(score_ref, scorenew_ref, sel_ref, selnew_ref, keys_scr, *, topk, idx_bits):
    rows, past = score_ref.shape
    ch = SAMPLE_TOPK_CHUNK
    n_ch = past // ch
    keys_scr[...] = _to_key(score_ref[...])
    key_new = _to_key(scorenew_ref[:, 0:1])
    lane = lax.broadcasted_iota(I32, (rows, ch), 1)
    one = lambda c: jnp.where(c, 1.0, 0.0)
    kf = float(topk)

    def count(pred, pred_new):
        acc = jnp.zeros((rows, 128), F32)
        for c in range(n_ch):
            acc = acc + _count_rows(one(pred(keys_scr[:, c * ch:(c + 1) * ch], lane + c * ch)))
        return jnp.sum(acc, axis=-1, keepdims=True) + one(pred_new)

    t0 = jnp.where(count(lambda k, p: k >= 0, key_new >= 0) >= kf, 0, INT_MIN).astype(I32)

    def bit_step(b, t):
        cand = t + jnp.left_shift(jnp.int32(1), 30 - b)
        return jnp.where(count(lambda k, p: k >= cand, key_new >= cand) >= kf, cand, t)

    thr = lax.fori_loop(0, 31, bit_step, t0)
    need = kf - count(lambda k, p: k > thr, key_new > thr)

    def pos_step(b, p_lo):
        cand = p_lo + jnp.left_shift(jnp.int32(1), idx_bits - 1 - b)
        c = count(lambda k, p: (k == thr) & (p < cand), (key_new == thr) & (past < cand))
        return jnp.where(c < need, cand, p_lo)

    pos = lax.fori_loop(0, idx_bits, pos_step, jnp.zeros((rows, 1), I32))
    for c in range(n_ch):
        k = keys_scr[:, c * ch:(c + 1) * ch]
        sel_ref[:, c * ch:(c + 1) * ch] = one((k > thr) | ((k == thr) & ((lane + c * ch) <= pos)))
    sel_new = one((key_new > thr) | ((key_new == thr) & (past <= pos)))
    selnew_ref[...] = jnp.broadcast_to(sel_new, selnew_ref.shape)


def _sample_topk(score, score_new):
    db, past = score.shape
    assert past % SAMPLE_TOPK_CHUNK == 0
    topk = min(TOPK_MAX, (past + 1) // 4)
    idx_bits = int(math.ceil(math.log2(past + 1)))
    kern = functools.partial(_sample_topk_kernel, topk=topk, idx_bits=idx_bits)
    return pl.pallas_call(
        kern,
        out_shape=[jax.ShapeDtypeStruct((db, past), F32), jax.ShapeDtypeStruct((db, 128), F32)],
        scratch_shapes=[pltpu.VMEM((db, past), I32)],
        compiler_params=pltpu.CompilerParams(vmem_limit_bytes=VMEM_LIMIT),
        name="sample_topk",
    )(score, score_new)


def _prep_inproj_weights(w_in_l, q_norm_a, k_norm_a, q_norm_b, k_norm_b):
    c_qi_end = 3 * WA + 3 * WB + WI
    c_misc_end = c_qi_end + D_IDX + H_IDX
    d = w_in_l.shape[0]
    pad = jnp.zeros((d, 128 - D_IDX - H_IDX), w_in_l.dtype)
    w1 = jnp.concatenate([w_in_l[:, :c_misc_end], pad], axis=1).astype(BF16)
    w_gate = w_in_l[:, c_misc_end:].astype(BF16)
    gains = jnp.stack([jnp.tile(q_norm_a, WA // HD_A), jnp.tile(k_norm_a, WA // HD_A),
                       jnp.tile(q_norm_b, H_B), jnp.tile(k_norm_b, H_B)], axis=0).astype(F32)
    grp = np.arange(WA) // HD_A
    g64 = jnp.asarray((grp[:, None] == grp[None, :]).astype(np.float32) / HD_A, dtype=BF16)
    return w1, w_gate, gains, g64


def kernel(x_prompt, x_sample, cache_diff_k, cache_diff_v, cache_dsa_k, cache_dsa_v, cache_idx_k, page_table, rel_bias_table, norm_mix, w_in, q_norm_a, k_norm_a, q_norm_b, k_norm_b, lambda_q1, lambda_k1, lambda_q2, lambda_k2, subln_a, w_proj_a, w_proj_b, w_out, norm_ffn, w_query, sub_keys, expert_u, expert_v):
    assert w_in.shape[0] == 1, "single-layer trunk"
    l = 0
    B, S, D = x_prompt.shape
    DB, DS, _ = x_sample.shape
    assert DS == 1
    n_pool = cache_diff_k.shape[1]
    n_pages = page_table.shape[1]
    bf = lambda a: a.astype(BF16)

    w1, w_gate, gains, g64 = _prep_inproj_weights(w_in[l], q_norm_a[l], k_norm_a[l], q_norm_b[l], k_norm_b[l])
    g_mix = norm_mix[l][None]
    g_ffn = norm_ffn[l][None]
    lam4 = jnp.stack([lambda_q1[l], lambda_k1[l], lambda_q2[l], lambda_k2[l]]).astype(F32)
    subln = subln_a[l][None]
    w_pa, w_pb, w_o, w_q = bf(w_proj_a[l]), bf(w_proj_b[l]), bf(w_out[l]), bf(w_query[l])
    sk = bf(sub_keys[l])
    eu = bf(expert_u[l])
    evt = _transpose_cast(expert_v[l], BF16, 1024)

    def channel_mix(x2, oa, ob, tt_merge, tt_peer):
        x1, h2, q = _merge(x2, oa, ob, g_mix, w_gate, w_pa, w_pb, w_o, g_ffn, w_q, tt_merge)
        rank1, pb, pn, pa = _peer_select(q, sk)
        return _peer_main(h2, x1, eu, evt, rank1, pb, pn, pa, tt_peer, 1024)

    xp2 = x_prompt.reshape(B * S, D)
    tq = 256
    (qa, kaf, kab, vaf, _, qb, kbf, kbb, vbf, _, qi, kif, kib, wi, va_t, vb_t) = _inproj(xp2, g_mix, w1, gains, g64, tq)
    bias_p = _prompt_bias_tiles(rel_bias_table, tq)
    r3 = lambda a: a.reshape(B, S, -1)
    wi_t = jnp.swapaxes(r3(wi), 1, 2)
    oa, ob = _prompt_attention(r3(qa), r3(qb), r3(qi), wi_t, r3(kab), va_t, r3(kbb), vb_t, r3(kib),
                               bias_p, lam4, subln, tq)
    y_p = channel_mix(xp2, oa.reshape(B * S, WA), ob.reshape(B * S, WB), 256, 512)

    xs2 = x_sample.reshape(DB, D)
    (qa_s, kaf_s, kab_s, vaf_s, vab_s, qb_s, kbf_s, kbb_s, vbf_s, vbb_s,
     qi_s, kif_s, kib_s, wi_s, _, _) = _inproj(xs2, g_mix, w1, gains, g64, DB)
    row = lambda a: a[:, None, :]
    page_view = lambda c: c.reshape(n_pool, PAGE_ROWS, 128)
    bias_a, bias_a_new = _sample_bias(rel_bias_table[:, :H_A], n_pages, 2)
    oa_s, score, score_new = _sample_attention(
        page_table, _map_rows(qa_s, 2, True), _map_rows(kab_s, 2, False), _map_rows(vab_s, 2, False),
        bias_a, bias_a_new, page_view(cache_diff_k), page_view(cache_diff_v), scale=1.0,
        index_args=(qi_s.reshape(DB, H_IDX, D_IDX), wi_s.reshape(DB, H_IDX, 1), row(kib_s), lam4, subln,
                    jnp.swapaxes(cache_idx_k.reshape(n_pool, PAGE_SIZE, D_IDX), 1, 2)))
    bias_b, bias_b_new = _sample_bias(rel_bias_table[:, H_A:], n_pages, 1)
    sel, sel_new = _sample_topk(score.reshape(DB, n_pages * PAGE_SIZE), score_new.reshape(DB, 128))
    sel_rows = jnp.repeat(sel.reshape(DB, n_pages, PAGE_SIZE), HEADS_PER_TOKEN, axis=-1)
    sel_new = sel_new.reshape(DB, 1, 128)
    (ob_s,) = _sample_attention(
        page_table, _map_rows(qb_s, 1, False), _map_rows(kbb_s, 1, False), _map_rows(vbb_s, 1, False),
        bias_b, bias_b_new, page_view(cache_dsa_k), page_view(cache_dsa_v), scale=HD_B ** -0.5,
        mask_args=(sel_rows, sel_new))
    heads_of = lambda o: o[:, :HEADS_PER_TOKEN, :].reshape(DB, HEADS_PER_TOKEN * 128).astype(BF16)
    y_s = channel_mix(xs2, heads_of(oa_s), heads_of(ob_s), DB, DB)

    return (y_p.reshape(B, S, D), y_s.reshape(DB, DS, D),
            kaf.reshape(1, B, S, H_A, 2 * HD_A), vaf.reshape(1, B, S, H_A, DV_A),
            kbf.reshape(1, B, S, H_B, HD_B), vbf.reshape(1, B, S, H_B, HD_B), kif.reshape(1, B, S, D_IDX),
            kaf_s.reshape(1, DB, DS, H_A, 2 * HD_A), vaf_s.reshape(1, DB, DS, H_A, DV_A),
            kbf_s.reshape(1, DB, DS, H_B, HD_B), vbf_s.reshape(1, DB, DS, H_B, HD_B),
            kif_s.reshape(1, DB, DS, D_IDX))
```
